```python
import jax, jax.numpy as jnp
from jax import lax
import numpy as np

D_MODEL = 1024
BATCH = 32
SEQ = 2048
DEPTH = 1
DEC_BATCH = 128
DEC_SEQ = 4
PAST_LEN = 8192
PAGE_SIZE = 128

D_MIX = D_MODEL
N_HEADS = 8
HEAD_DIM = 64
D_ATTN = N_HEADS * HEAD_DIM
CONV_CH = D_MIX - D_ATTN
CONV_WIDTH = 31
D_IN = 3 * D_ATTN + 2 * CONV_CH
N_EXPERTS = 32
TOP_K = 4
D_FF = D_MODEL
SWIGLU_LIMIT = 7.0
SWIGLU_ALPHA = 1.702
SB_BIAS_INIT = -8.0
Q_BLOCK = 128
MOE_BLOCK = 128
EPS = 1e-5

kernel_name = "hymba_stickbreak_conformer_moe_adaln_step"


def _rmsnorm(x, g):
    xf = x.astype(jnp.float32)
    y = xf * lax.rsqrt(jnp.mean(xf * xf, axis=-1, keepdims=True) + EPS)
    return (y * g.astype(jnp.float32)).astype(x.dtype)


def _layernorm(x, g, b):
    xf = x.astype(jnp.float32)
    mu = jnp.mean(xf, axis=-1, keepdims=True)
    var = jnp.mean(jnp.square(xf - mu), axis=-1, keepdims=True)
    y = (xf - mu) * lax.rsqrt(var + EPS)
    return (y * g.astype(jnp.float32) + b.astype(jnp.float32)).astype(x.dtype)


def _stick_breaking(z, q_pos, k_pos):
    valid = k_pos[None, :] < q_pos[:, None]
    m = jnp.where(valid, jax.nn.log_sigmoid(-z), 0.0)
    after = lax.cumsum(m, axis=3, reverse=True) - m
    return jnp.where(valid, jnp.exp(jax.nn.log_sigmoid(z) + after), 0.0)


def _attn_prompt(q, k, v, sb_bias):
    B, S, H, Dh = q.shape
    nb = S // Q_BLOCK
    scale = Dh ** -0.5
    bias = sb_bias.astype(jnp.float32)[None, :, None, None]
    qb = jnp.moveaxis(q.reshape(B, nb, Q_BLOCK, H, Dh), 1, 0)
    k_pos = jnp.arange(S)

    def block(args):
        qi, i = args
        q_pos = i * Q_BLOCK + jnp.arange(Q_BLOCK)
        z = jnp.einsum('bqhd,bkhd->bhqk', qi, k).astype(jnp.float32) * scale + bias
        a = _stick_breaking(z, q_pos, k_pos)
        return jnp.einsum('bhqk,bkhd->bqhd', a.astype(v.dtype), v)

    o = lax.map(block, (qb, jnp.arange(nb)))
    return jnp.moveaxis(o, 0, 1).reshape(B, S, H * Dh)


def _attn_sample(q, k, v, sb_bias, k_past, v_past):
    B, Qn, H, Dh = q.shape
    P = k_past.shape[1]
    scale = Dh ** -0.5
    bias = sb_bias.astype(jnp.float32)[None, :, None, None]
    z = jnp.concatenate([jnp.einsum('bqhd,bkhd->bhqk', q, k_past),
                         jnp.einsum('bqhd,bkhd->bhqk', q, k)], axis=3).astype(jnp.float32) * scale + bias
    a = _stick_breaking(z, P + jnp.arange(Qn), jnp.arange(P + Qn)).astype(v.dtype)
    o = jnp.einsum('bhqk,bkhd->bqhd', a[..., :P], v_past) + jnp.einsum('bhqk,bkhd->bqhd', a[..., P:], v)
    return o.reshape(B, Qn, H * Dh)


def _conformer_conv(u, buf, w_dw, b_dw, ln_g, ln_b):
    glu = u[..., :CONV_CH] * jax.nn.sigmoid(u[..., CONV_CH:])
    ext = jnp.concatenate([buf, glu], axis=1)
    y = lax.conv_general_dilated(ext, w_dw[:, None, :].astype(ext.dtype), (1,), 'VALID',
                                 dimension_numbers=('NWC', 'WIO', 'NWC'),
                                 feature_group_count=CONV_CH) + b_dw
    y = jax.nn.silu(_layernorm(y, ln_g, ln_b))
    return y, ext[:, -(CONV_WIDTH - 1):]


def _moe(h, w_router, b_router, w_gate_up, b_gate_up, w_down, b_down):
    T, D = h.shape
    TK = T * TOP_K
    n_blocks = -(-TK // MOE_BLOCK) + N_EXPERTS
    logits = (h @ w_router).astype(jnp.float32) + b_router.astype(jnp.float32)
    top_val, top_idx = lax.top_k(logits, TOP_K)
    gates = jax.nn.softmax(top_val, axis=-1)
    flat_e = top_idx.reshape(-1)
    order = jnp.argsort(flat_e, stable=True)
    sorted_e = flat_e[order]
    sorted_tok = (order // TOP_K).astype(jnp.int32)
    counts = jnp.bincount(flat_e, length=N_EXPERTS)
    padded = (counts + MOE_BLOCK - 1) // MOE_BLOCK * MOE_BLOCK
    pend = jnp.cumsum(padded)
    pstart = pend - padded
    start = jnp.cumsum(counts) - counts
    dest = pstart[sorted_e] + jnp.arange(TK) - start[sorted_e]
    buf_tok = jnp.full((n_blocks * MOE_BLOCK,), T, jnp.int32).at[dest].set(sorted_tok)
    block_e = jnp.minimum(jnp.searchsorted(pend, jnp.arange(n_blocks) * MOE_BLOCK, side='right'),
                          N_EXPERTS - 1)
    h_pad = jnp.concatenate([h, jnp.zeros((1, D), h.dtype)], axis=0)
    x_blocks = h_pad[buf_tok].reshape(n_blocks, MOE_BLOCK, D)

    def expert_block(args):
        xb, e = args
        gu = xb @ w_gate_up[e] + b_gate_up[e]
        gate = jnp.minimum(gu[:, :D_FF], SWIGLU_LIMIT)
        up = jnp.clip(gu[:, D_FF:], -SWIGLU_LIMIT, SWIGLU_LIMIT)
        act = (up + 1) * (gate * jax.nn.sigmoid(SWIGLU_ALPHA * gate))
        return act @ w_down[e] + b_down[e]

    y_blocks = lax.map(expert_block, (x_blocks, block_e)).reshape(n_blocks * MOE_BLOCK, D)
    y_sorted = y_blocks[dest] * gates.reshape(-1)[order][:, None].astype(h.dtype)
    return jax.ops.segment_sum(y_sorted, sorted_tok, num_segments=T)


def _layer(x, c, attend, conv_buf, g_attn_norm, g_ffn_norm, w_ada, b_ada, w_in, sb_bias, w_dw, b_dw,
           ln_conv_g, ln_conv_b, w_out, w_router, b_router, w_gate_up, b_gate_up, w_down, b_down):
    B, S, D = x.shape
    ada = jax.nn.silu(c) @ w_ada + b_ada
    sh1, sc1, ga1, sh2, sc2, ga2 = [a[:, None, :] for a in jnp.split(ada, 6, axis=-1)]
    h = _rmsnorm(x, g_attn_norm) * (1 + sc1) + sh1
    proj = h @ w_in
    q = proj[..., :D_ATTN].reshape(B, S, N_HEADS, HEAD_DIM)
    k = proj[..., D_ATTN:2 * D_ATTN].reshape(B, S, N_HEADS, HEAD_DIM)
    v = proj[..., 2 * D_ATTN:3 * D_ATTN].reshape(B, S, N_HEADS, HEAD_DIM)
    attn = attend(q, k, v, sb_bias)
    conv, new_buf = _conformer_conv(proj[..., 3 * D_ATTN:], conv_buf, w_dw, b_dw, ln_conv_g, ln_conv_b)
    x = x + ga1 * (jnp.concatenate([attn, conv], axis=-1) @ w_out)
    h2 = _rmsnorm(x, g_ffn_norm) * (1 + sc2) + sh2
    ff = _moe(h2.reshape(B * S, D), w_router, b_router, w_gate_up, b_gate_up, w_down, b_down)
    x = x + ga2 * ff.reshape(B, S, D)
    return x, k, v, new_buf


def setup_inputs(seed: int = 0) -> dict:
    key = jax.random.key(seed)
    ks = jax.random.split(key, 32)
    n_pages = PAST_LEN // PAGE_SIZE
    n_used = DEC_BATCH * n_pages
    n_pool = n_used + n_used // 4
    f = jnp.float32
    nrm = lambda k, shape, s=1.0: jax.random.normal(k, shape, f) * s
    page_table = jax.random.permutation(ks[0], n_pool)[:n_used].reshape(DEC_BATCH, n_pages).astype(jnp.int32)
    return {
        "x_prompt": nrm(ks[1], (BATCH, SEQ, D_MODEL)),
        "x_sample": nrm(ks[2], (DEC_BATCH, DEC_SEQ, D_MODEL)),
        "c_prompt": nrm(ks[3], (BATCH, D_MODEL)),
        "c_sample": nrm(ks[4], (DEC_BATCH, D_MODEL)),
        "cache_k": nrm(ks[5], (DEPTH, n_pool, PAGE_SIZE, N_HEADS, HEAD_DIM)),
        "cache_v": nrm(ks[6], (DEPTH, n_pool, PAGE_SIZE, N_HEADS, HEAD_DIM)),
        "state_conv": nrm(ks[7], (DEPTH, DEC_BATCH, CONV_WIDTH - 1, CONV_CH), 0.5),
        "page_table": page_table,
        "g_attn_norm": 1.0 + nrm(ks[8], (DEPTH, D_MODEL), 0.01),
        "g_ffn_norm": 1.0 + nrm(ks[9], (DEPTH, D_MODEL), 0.01),
        "w_ada": nrm(ks[10], (DEPTH, D_MODEL, 6 * D_MODEL), D_MODEL ** -0.5),
        "b_ada": nrm(ks[11], (DEPTH, 6 * D_MODEL), 0.01),
        "w_in": nrm(ks[12], (DEPTH, D_MODEL, D_IN), D_MODEL ** -0.5),
        "sb_bias": SB_BIAS_INIT + nrm(ks[25], (DEPTH, N_HEADS), 0.1),
        "w_dw": nrm(ks[13], (DEPTH, CONV_WIDTH, CONV_CH), CONV_WIDTH ** -0.5),
        "b_dw": nrm(ks[14], (DEPTH, CONV_CH), 0.01),
        "ln_conv_g": 1.0 + nrm(ks[15], (DEPTH, CONV_CH), 0.01),
        "ln_conv_b": nrm(ks[16], (DEPTH, CONV_CH), 0.01),
        "w_out": nrm(ks[17], (DEPTH, D_MIX, D_MODEL), D_MIX ** -0.5),
        "w_router": nrm(ks[18], (DEPTH, D_MODEL, N_EXPERTS), D_MODEL ** -0.5),
        "b_router": nrm(ks[19], (DEPTH, N_EXPERTS), 0.01),
        "w_gate_up": nrm(ks[20], (DEPTH, N_EXPERTS, D_MODEL, 2 * D_FF), D_MODEL ** -0.5),
        "b_gate_up": nrm(ks[21], (DEPTH, N_EXPERTS, 2 * D_FF), 0.01),
        "w_down": nrm(ks[22], (DEPTH, N_EXPERTS, D_FF, D_MODEL), D_FF ** -0.5),
        "b_down": nrm(ks[23], (DEPTH, N_EXPERTS, D_MODEL), 0.01),
        "g_final": 1.0 + nrm(ks[24], (D_MODEL,), 0.01),
    }


def reference(x_prompt, x_sample, c_prompt, c_sample, cache_k, cache_v, state_conv, page_table,
              g_attn_norm, g_ffn_norm, w_ada, b_ada, w_in, sb_bias, w_dw, b_dw, ln_conv_g, ln_conv_b,
              w_out, w_router, b_router, w_gate_up, b_gate_up, w_down, b_down, g_final):
    n_batch, n_pages = page_table.shape
    past = n_pages * PAGE_SIZE
    xp, xs = x_prompt, x_sample
    kp_rows, vp_rows, conv_p, ks_rows, vs_rows, conv_s = [], [], [], [], [], []
    for l in range(DEPTH):
        w = (g_attn_norm[l], g_ffn_norm[l], w_ada[l], b_ada[l], w_in[l], sb_bias[l], w_dw[l], b_dw[l],
             ln_conv_g[l], ln_conv_b[l], w_out[l], w_router[l], b_router[l], w_gate_up[l],
             b_gate_up[l], w_down[l], b_down[l])
        buf0 = jnp.zeros((xp.shape[0], CONV_WIDTH - 1, CONV_CH), xp.dtype)
        xp, kp, vp, bp = _layer(xp, c_prompt, _attn_prompt, buf0, *w)
        k_past = cache_k[l][page_table].reshape(n_batch, past, N_HEADS, HEAD_DIM)
        v_past = cache_v[l][page_table].reshape(n_batch, past, N_HEADS, HEAD_DIM)
        attend_s = lambda q, k, v, b, k_past=k_past, v_past=v_past: _attn_sample(q, k, v, b, k_past, v_past)
        xs, ks_, vs_, bs = _layer(xs, c_sample, attend_s, state_conv[l], *w)
        kp_rows.append(kp); vp_rows.append(vp); conv_p.append(bp)
        ks_rows.append(ks_); vs_rows.append(vs_); conv_s.append(bs)
    y_prompt = _rmsnorm(xp, g_final)
    y_sample = _rmsnorm(xs, g_final)
    return (y_prompt, y_sample, jnp.stack(kp_rows), jnp.stack(vp_rows), jnp.stack(conv_p),
            jnp.stack(ks_rows), jnp.stack(vs_rows), jnp.stack(conv_s))
```

```python
import functools

import jax
import jax.numpy as jnp
from jax import lax
from jax.experimental import pallas as pl
from jax.experimental.pallas import tpu as pltpu

F32 = jnp.float32
BF16 = jnp.bfloat16

D_MODEL = 1024
N_HEADS = 8
HEAD_DIM = 64
D_ATTN = N_HEADS * HEAD_DIM
CONV_CH = D_MODEL - D_ATTN
CONV_WIDTH = 31
CONV_HIST = CONV_WIDTH - 1
D_IN = 3 * D_ATTN + 2 * CONV_CH
N_EXPERTS = 32
TOP_K = 4
D_FF = D_MODEL
SWIGLU_LIMIT = 7.0
SWIGLU_ALPHA = 1.702
EPS = 1e-5
PAGE_SIZE = 128

LANES = 128
V7X_VMEM_BYTES = 64 * 1024 * 1024
VMEM_LIMIT = 48 * 1024 * 1024

ROW_TILE = 512
ATTN_TILE = 256
CONV_TILE = 256
ROUTE_TILE = 512
MOE_TILE = 512
COMBINE_TILE = 256
PAGES_PER_STEP = 8
NEG_BIG = -1e30


def _params(*sem):
    return pltpu.CompilerParams(dimension_semantics=sem, vmem_limit_bytes=VMEM_LIMIT)


def _sigmoid(x):
    return 1.0 / (1.0 + jnp.exp(-x))


def _log_sigmoid_pair(z):
    l = jnp.log(1.0 + jnp.exp(-jnp.abs(z)))
    return -(jnp.maximum(z, 0.0) + l), jnp.minimum(z, 0.0) - l


def _split_bf16(x):
    hi = x.astype(BF16)
    lo = (x - hi.astype(F32)).astype(BF16)
    return hi, lo


def _strict_upper(n):
    r = lax.broadcasted_iota(jnp.int32, (n, n), 0)
    c = lax.broadcasted_iota(jnp.int32, (n, n), 1)
    return jnp.where(r > c, 1.0, 0.0).astype(BF16)


def _dot(a, b):
    return jnp.dot(a, b, preferred_element_type=F32)


def _dot_t(a, b):
    return lax.dot_general(a, b, (((1,), (1,)), ((), ())), preferred_element_type=F32)


def _ada_kernel(c_ref, w_ref, b_ref, o_ref):
    c = c_ref[...]
    s = (c * _sigmoid(c)).astype(BF16)
    o_ref[...] = _dot(s, w_ref[...].astype(BF16)) + b_ref[...]


def _ada(c_all, w_ada, b_ada):
    n, d = c_all.shape
    nout = w_ada.shape[1]
    tn = 1024
    return pl.pallas_call(
        _ada_kernel,
        grid=(nout // tn,),
        in_specs=[pl.BlockSpec((n, d), lambda j: (0, 0)),
                  pl.BlockSpec((d, tn), lambda j: (0, j)),
                  pl.BlockSpec((1, tn), lambda j: (0, j))],
        out_specs=pl.BlockSpec((n, tn), lambda j: (0, j)),
        out_shape=jax.ShapeDtypeStruct((n, nout), F32),
        compiler_params=_params("arbitrary"),
        name="ada",
    )(c_all, w_ada, b_ada.reshape(1, nout))


def _inproj_kernel(x_ref, sc_ref, sh_ref, g_ref, w_ref,
                   q_ref, k_ref, v_ref, kb_ref, vb_ref, glu_ref):
    x = x_ref[0]
    ms = jnp.mean(x * x, axis=-1, keepdims=True)
    h = x * lax.rsqrt(ms + EPS) * g_ref[...]
    h = (h * (1.0 + sc_ref[0]) + sh_ref[0]).astype(BF16)
    c = D_ATTN
    q_ref[0] = (_dot(h, w_ref[:, 0:c]) * (HEAD_DIM ** -0.5)).astype(BF16)
    k = _dot(h, w_ref[:, c:2 * c])
    k_ref[0] = k
    kb_ref[0] = k.astype(BF16)
    v = _dot(h, w_ref[:, 2 * c:3 * c])
    v_ref[0] = v
    vb_ref[0] = v.astype(BF16)
    a = _dot(h, w_ref[:, 3 * c:3 * c + CONV_CH])
    g = _dot(h, w_ref[:, 3 * c + CONV_CH:])
    glu_ref[0] = a * _sigmoid(g)


def _mod_spec(mod, ts):
    if mod.shape[1] == 1:
        return pl.BlockSpec((1, 1, D_MODEL), lambda g, t: (g, 0, 0))
    return pl.BlockSpec((1, ts, D_MODEL), lambda g, t: (g, t, 0))


def _inproj(x, sc, sh, g_norm, w_in_bf):
    G, R, D = x.shape
    ts = min(ROW_TILE, R)
    tok = lambda n: pl.BlockSpec((1, ts, n), lambda g, t: (g, t, 0))
    shp = lambda n, dt: jax.ShapeDtypeStruct((G, R, n), dt)
    return pl.pallas_call(
        _inproj_kernel,
        grid=(G, R // ts),
        in_specs=[tok(D), _mod_spec(sc, ts), _mod_spec(sh, ts),
                  pl.BlockSpec((1, D), lambda g, t: (0, 0)),
                  pl.BlockSpec((D, D_IN), lambda g, t: (0, 0))],
        out_specs=[tok(D_ATTN)] * 5 + [tok(CONV_CH)],
        out_shape=[shp(D_ATTN, BF16), shp(D_ATTN, F32), shp(D_ATTN, F32),
                   shp(D_ATTN, BF16), shp(D_ATTN, BF16), shp(CONV_CH, F32)],
        compiler_params=_params("arbitrary", "arbitrary"),
        name="inproj",
    )(x, sc, sh, g_norm.reshape(1, D), w_in_bf)


def _sb_tile(qh, kblk, vblk, bias, carry, upper, valid):
    z = _dot_t(qh, kblk) + bias
    m, lsz = _log_sigmoid_pair(z)
    if valid is not None:
        m = jnp.where(valid, m, 0.0)
    m_hi, m_lo = _split_bf16(m)
    after = _dot(m_hi, upper) + _dot(m_lo, upper)
    a = jnp.exp(lsz + after + carry)
    if valid is not None:
        a = jnp.where(valid, a, 0.0)
    return _dot(a.astype(BF16), vblk), jnp.sum(m, axis=-1, keepdims=True)


def _attn_prompt_kernel(bias_ref, q_ref, k_ref, v_ref, o_ref, acc_e, acc_o, car_e, car_o):
    t = ATTN_TILE
    p = pl.program_id(1)
    qi = pl.program_id(2)
    q2 = q_ref[0]
    lane = lax.broadcasted_iota(jnp.int32, (1, LANES), 1)
    even = lane < HEAD_DIM
    q_e = jnp.where(even, q2, jnp.zeros_like(q2))
    q_o = jnp.where(even, jnp.zeros_like(q2), q2)
    b_e = bias_ref[2 * p]
    b_o = bias_ref[2 * p + 1]
    upper = _strict_upper(t)

    r = lax.broadcasted_iota(jnp.int32, (t, t), 0)
    c = lax.broadcasted_iota(jnp.int32, (t, t), 1)
    valid = c < r
    start = pl.multiple_of(qi * t, t)
    kblk = k_ref[0, pl.ds(start, t), :]
    vblk = v_ref[0, pl.ds(start, t), :]
    zero = jnp.zeros((t, 1), F32)
    oe, se = _sb_tile(q_e, kblk, vblk, b_e, zero, upper, valid)
    oo, so = _sb_tile(q_o, kblk, vblk, b_o, zero, upper, valid)
    acc_e[...] = oe
    acc_o[...] = oo
    car_e[...] = se
    car_o[...] = so

    def body(i, _):
        kb = qi - 1 - i
        st = pl.multiple_of(kb * t, t)
        kblk = k_ref[0, pl.ds(st, t), :]
        vblk = v_ref[0, pl.ds(st, t), :]
        oe, se = _sb_tile(q_e, kblk, vblk, b_e, car_e[...], upper, None)
        oo, so = _sb_tile(q_o, kblk, vblk, b_o, car_o[...], upper, None)
        acc_e[...] += oe
        acc_o[...] += oo
        car_e[...] += se
        car_o[...] += so
        return 0

    lax.fori_loop(0, qi, body, 0)
    o_ref[0] = jnp.where(even, acc_e[...], acc_o[...]).astype(o_ref.dtype)


def _attn_prompt(q, kb, vb, sb_bias):
    B, S, _ = q.shape
    t = ATTN_TILE
    npair = N_HEADS // 2
    return pl.pallas_call(
        _attn_prompt_kernel,
        grid_spec=pltpu.PrefetchScalarGridSpec(
            num_scalar_prefetch=0,
            grid=(B, npair, S // t),
            in_specs=[pl.BlockSpec(memory_space=pltpu.SMEM),
                      pl.BlockSpec((1, t, LANES), lambda b, p, i: (b, i, p)),
                      pl.BlockSpec((1, S, LANES), lambda b, p, i: (b, 0, p)),
                      pl.BlockSpec((1, S, LANES), lambda b, p, i: (b, 0, p))],
            out_specs=pl.BlockSpec((1, t, LANES), lambda b, p, i: (b, i, p)),
            scratch_shapes=[pltpu.VMEM((t, LANES), F32), pltpu.VMEM((t, LANES), F32),
                            pltpu.VMEM((t, 1), F32), pltpu.VMEM((t, 1), F32)]),
        out_shape=jax.ShapeDtypeStruct((B, S, D_ATTN), BF16),
        compiler_params=_params("arbitrary", "arbitrary", "arbitrary"),
        name="attn_prompt",
    )(sb_bias, q, kb, vb)


CONV_PAD = 32
CONV_TIME = 64


def _conv_prompt_kernel(glu_ref, w_ref, b_ref, g_ref, beta_ref, o_ref, ext_ref, y_ref):
    S = glu_ref.shape[1]
    tt = CONV_TIME
    ext_ref[0:CONV_PAD, :] = jnp.zeros((CONV_PAD, CONV_CH), F32)
    ext_ref[CONV_PAD:CONV_PAD + S, :] = glu_ref[0]
    off = CONV_PAD - CONV_HIST

    def conv_tile(i, _):
        t0 = pl.multiple_of(i * tt, tt)
        for cb in range(CONV_CH // LANES):
            cs = slice(cb * LANES, (cb + 1) * LANES)
            win = ext_ref[pl.ds(t0, tt + CONV_PAD), cs]
            acc = jnp.zeros((tt, LANES), F32) + b_ref[:, cs]
            for w in range(CONV_WIDTH):
                acc = acc + win[off + w:off + w + tt, :] * w_ref[w:w + 1, cs]
            y_ref[pl.ds(t0, tt), cs] = acc
        return 0

    lax.fori_loop(0, S // tt, conv_tile, 0)

    def norm_tile(i, _):
        t0 = pl.multiple_of(i * tt, tt)
        acc = y_ref[pl.ds(t0, tt), :]
        mu = jnp.mean(acc, axis=-1, keepdims=True)
        d = acc - mu
        var = jnp.mean(d * d, axis=-1, keepdims=True)
        y = d * lax.rsqrt(var + EPS) * g_ref[...] + beta_ref[...]
        o_ref[0, pl.ds(t0, tt), :] = (y * _sigmoid(y)).astype(o_ref.dtype)
        return 0

    lax.fori_loop(0, S // tt, norm_tile, 0)


def _conv_prompt(glu, w_dw, b_dw, ln_g, ln_b):
    B, S, C = glu.shape
    row = lambda: pl.BlockSpec((1, C), lambda b: (0, 0))
    return pl.pallas_call(
        _conv_prompt_kernel,
        grid=(B,),
        in_specs=[pl.BlockSpec((1, S, C), lambda b: (b, 0, 0)),
                  pl.BlockSpec((CONV_WIDTH, C), lambda b: (0, 0)),
                  row(), row(), row()],
        out_specs=pl.BlockSpec((1, S, C), lambda b: (b, 0, 0)),
        out_shape=jax.ShapeDtypeStruct((B, S, C), BF16),
        scratch_shapes=[pltpu.VMEM((S + CONV_PAD, C), F32), pltpu.VMEM((S, C), F32)],
        compiler_params=_params("arbitrary"),
        name="conv_prompt",
    )(glu, w_dw, b_dw.reshape(1, C), ln_g.reshape(1, C), ln_b.reshape(1, C))


def _conv_sample_kernel(st_ref, glu_ref, w_ref, b_ref, g_ref, beta_ref, o_ref, nst_ref):
    nq = glu_ref.shape[0]

    def ext(j):
        return st_ref[j] if j < CONV_HIST else glu_ref[j - CONV_HIST]

    for i in range(nq):
        acc = jnp.zeros(st_ref.shape[1:], F32) + b_ref[...]
        for w in range(CONV_WIDTH):
            acc = acc + ext(i + w) * w_ref[w:w + 1, :]
        mu = jnp.mean(acc, axis=-1, keepdims=True)
        d = acc - mu
        var = jnp.mean(d * d, axis=-1, keepdims=True)
        y = d * lax.rsqrt(var + EPS) * g_ref[...] + beta_ref[...]
        o_ref[i] = (y * _sigmoid(y)).astype(o_ref.dtype)
    for j in range(CONV_HIST):
        nst_ref[j] = ext(j + nq)


def _conv_sample(state_t, glu_t, w_dw, b_dw, ln_g, ln_b):
    H, Bs, C = state_t.shape
    nq = glu_t.shape[0]
    full = lambda shape: pl.BlockSpec(shape, lambda i: (0,) * len(shape))
    return pl.pallas_call(
        _conv_sample_kernel,
        grid=(1,),
        in_specs=[full((H, Bs, C)), full((nq, Bs, C)), full((CONV_WIDTH, C)),
                  full((1, C)), full((1, C)), full((1, C))],
        out_specs=[full((nq, Bs, C)), full((H, Bs, C))],
        out_shape=[jax.ShapeDtypeStruct((nq, Bs, C), BF16),
                   jax.ShapeDtypeStruct((H, Bs, C), F32)],
        compiler_params=_params("arbitrary"),
        name="conv_sample",
    )(state_t, glu_t, w_dw, b_dw.reshape(1, C), ln_g.reshape(1, C), ln_b.reshape(1, C))


def _attn_sample_kernel(pt_ref, q_ref, kn_ref, vn_ref, bias_ref, *rest):
    G = PAGES_PER_STEP
    k_refs = rest[:G]
    v_refs = rest[G:2 * G]
    o_ref = rest[2 * G]
    acc_ref, car_ref = rest[2 * G + 1:]
    s = pl.program_id(1)
    nq = q_ref.shape[1]
    rows = nq * N_HEADS

    hrow = lax.broadcasted_iota(jnp.int32, (N_HEADS, D_ATTN), 0)
    hcol = lax.broadcasted_iota(jnp.int32, (N_HEADS, D_ATTN), 1) // HEAD_DIM
    headmask = hrow == hcol
    q = q_ref[0]
    qbd = jnp.concatenate(
        [jnp.where(headmask, q[i:i + 1, :], 0.0) for i in range(nq)], axis=0).astype(BF16)
    bias = bias_ref[...]
    upper = _strict_upper(PAGE_SIZE)

    def page(kt, vt, valid):
        z = _dot(qbd, kt) + bias
        m, lsz = _log_sigmoid_pair(z)
        if valid is not None:
            m = jnp.where(valid, m, 0.0)
        m_hi, m_lo = _split_bf16(m)
        after = _dot(m_hi, upper) + _dot(m_lo, upper)
        a = jnp.exp(lsz + after + car_ref[...])
        if valid is not None:
            a = jnp.where(valid, a, 0.0)
        acc_ref[...] += _dot_t(a.astype(BF16), vt)
        car_ref[...] += jnp.sum(m, axis=-1, keepdims=True)

    @pl.when(s == 0)
    def _():
        acc_ref[...] = jnp.zeros_like(acc_ref)
        car_ref[...] = jnp.zeros_like(car_ref)
        r = lax.broadcasted_iota(jnp.int32, (rows, PAGE_SIZE), 0) // N_HEADS
        c = lax.broadcasted_iota(jnp.int32, (rows, PAGE_SIZE), 1)
        page(kn_ref[0], vn_ref[0], c < r)

    for j in range(G):
        kt = k_refs[j][0].reshape(D_ATTN, PAGE_SIZE).astype(BF16)
        vt = v_refs[j][0].reshape(D_ATTN, PAGE_SIZE).astype(BF16)
        page(kt, vt, None)

    @pl.when(s == pl.num_programs(1) - 1)
    def _():
        acc = acc_ref[...]
        out = [jnp.sum(jnp.where(headmask, acc[i * N_HEADS:(i + 1) * N_HEADS, :], 0.0),
                       axis=0, keepdims=True) for i in range(nq)]
        o_ref[0] = jnp.concatenate(out, axis=0).astype(o_ref.dtype)


def _attn_sample(q, knew_t, vnew_t, bias_rows, cache_kt, cache_vt, page_table):
    Bs, nq, _ = q.shape
    n_pages = page_table.shape[1]
    G = PAGES_PER_STEP
    assert n_pages % G == 0
    rows = nq * N_HEADS

    def page_spec(j):
        return pl.BlockSpec(
            (1, N_HEADS, HEAD_DIM, PAGE_SIZE),
            lambda b, s, pt, j=j: (pt[b, n_pages - 1 - (s * G + j)], 0, 0, 0))

    per_b = lambda shape: pl.BlockSpec((1,) + shape, lambda b, s, pt: (b, 0, 0))
    return pl.pallas_call(
        _attn_sample_kernel,
        grid_spec=pltpu.PrefetchScalarGridSpec(
            num_scalar_prefetch=1,
            grid=(Bs, n_pages // G),
            in_specs=[per_b((nq, D_ATTN)), per_b((D_ATTN, PAGE_SIZE)), per_b((D_ATTN, PAGE_SIZE)),
                      pl.BlockSpec((rows, 1), lambda b, s, pt: (0, 0))]
                     + [page_spec(j) for j in range(G)] * 2,
            out_specs=per_b((nq, D_ATTN)),
            scratch_shapes=[pltpu.VMEM((rows, D_ATTN), F32), pltpu.VMEM((rows, 1), F32)]),
        out_shape=jax.ShapeDtypeStruct((Bs, nq, D_ATTN), F32),
        compiler_params=_params("arbitrary", "arbitrary"),
        name="attn_sample",
    )(page_table, q, knew_t, vnew_t, bias_rows, *([cache_kt] * G), *([cache_vt] * G))


def _outproj_kernel(x_ref, a_ref, c_ref, ga_ref, sc_ref, sh_ref, g_ref, w_ref,
                    wr_hi_ref, wr_lo_ref, br_ref, x1_ref, h2_ref, lg_ref):
    mix = _dot(a_ref[0], w_ref[0:D_ATTN, :]) + _dot(c_ref[0], w_ref[D_ATTN:, :])
    x1 = x_ref[0] + ga_ref[0] * mix
    x1_ref[0] = x1
    ms = jnp.mean(x1 * x1, axis=-1, keepdims=True)
    h2 = x1 * lax.rsqrt(ms + EPS) * g_ref[...]
    h2 = h2 * (1.0 + sc_ref[0]) + sh_ref[0]
    h2_ref[0] = h2
    hi, lo = _split_bf16(h2)
    lg_ref[0] = (_dot(hi, wr_hi_ref[...]) + _dot(lo, wr_hi_ref[...])
                 + _dot(hi, wr_lo_ref[...]) + br_ref[...])


def _outproj(x, attn, conv, ga, sc, sh, g_norm, w_out_bf, wr_hi, wr_lo, br_pad):
    G, R, D = x.shape
    ts = min(ROW_TILE, R)
    tok = lambda n: pl.BlockSpec((1, ts, n), lambda g, t: (g, t, 0))
    const = lambda shape: pl.BlockSpec(shape, lambda g, t: (0, 0))
    return pl.pallas_call(
        _outproj_kernel,
        grid=(G, R // ts),
        in_specs=[tok(D), tok(D_ATTN), tok(CONV_CH), _mod_spec(ga, ts), _mod_spec(sc, ts),
                  _mod_spec(sh, ts), const((1, D)), const((D, D)),
                  const((D, LANES)), const((D, LANES)), const((1, LANES))],
        out_specs=[tok(D), tok(D), tok(LANES)],
        out_shape=[jax.ShapeDtypeStruct((G, R, D), F32), jax.ShapeDtypeStruct((G, R, D), F32),
                   jax.ShapeDtypeStruct((G, R, LANES), F32)],
        compiler_params=_params("arbitrary", "arbitrary"),
        name="outproj",
    )(x, attn, conv, ga, sc, sh, g_norm.reshape(1, D), w_out_bf, wr_hi, wr_lo, br_pad)


def _route_kernel(lg_ref, dest_ref, gate_ref, cnt_ref, counts, offs, run):
    ph = pl.program_id(0)
    t = pl.program_id(1)
    tr = lg_ref.shape[0]
    lane = lax.broadcasted_iota(jnp.int32, (tr, LANES), 1).astype(F32)
    l = lg_ref[...]
    onehots, vals = [], []
    for _ in range(TOP_K):
        mx = jnp.max(l, axis=-1, keepdims=True)
        idx = jnp.min(jnp.where(l == mx, lane, float(LANES)), axis=-1, keepdims=True)
        oh = lane == idx
        onehots.append(oh)
        vals.append(mx)
        l = jnp.where(oh, NEG_BIG, l)
    total = jnp.zeros((tr, LANES), F32)
    for oh in onehots:
        total = total + jnp.where(oh, 1.0, 0.0)
    colsum = jnp.sum(total, axis=0, keepdims=True)

    @pl.when(jnp.logical_and(ph == 0, t == 0))
    def _():
        counts[...] = jnp.zeros_like(counts)

    @pl.when(ph == 0)
    def _():
        counts[...] += colsum

    @pl.when(jnp.logical_and(ph == 1, t == 0))
    def _():
        c = jnp.broadcast_to(counts[...], (8, LANES))
        c1 = c.astype(BF16)
        r1 = c - c1.astype(F32)
        c2 = r1.astype(BF16)
        c3 = (r1 - c2.astype(F32)).astype(BF16)
        rr = lax.broadcasted_iota(jnp.int32, (LANES, LANES), 0)
        cc = lax.broadcasted_iota(jnp.int32, (LANES, LANES), 1)
        before = jnp.where(rr < cc, 1.0, 0.0).astype(BF16)
        o = _dot(c1, before) + _dot(c2, before) + _dot(c3, before)
        offs[...] = o[0:1, :]
        run[...] = jnp.zeros_like(run)
        cnt_ref[...] = counts[...]

    @pl.when(ph == 1)
    def _():
        rr = lax.broadcasted_iota(jnp.int32, (tr, tr), 0)
        cc = lax.broadcasted_iota(jnp.int32, (tr, tr), 1)
        earlier = jnp.where(cc < rr, 1.0, 0.0).astype(BF16)
        pos = _dot(earlier, total.astype(BF16)) + (offs[...] + run[...])
        kcol = lax.broadcasted_iota(jnp.int32, (tr, LANES), 1)
        dest = jnp.zeros((tr, LANES), F32)
        gate = jnp.zeros((tr, LANES), F32)
        es = [jnp.exp(v - vals[0]) for v in vals]
        den = es[0] + es[1] + es[2] + es[3]
        for k in range(TOP_K):
            dk = jnp.sum(jnp.where(onehots[k], pos, 0.0), axis=-1, keepdims=True)
            dest = jnp.where(kcol == k, dk, dest)
            gate = jnp.where(kcol == k, es[k] / den, gate)
        dest_ref[...] = dest[:, 0:TOP_K].astype(jnp.int32)
        gate_ref[...] = gate[:, 0:TOP_K]
        run[...] += colsum


def _route(logits):
    T = logits.shape[0]
    tr = min(ROUTE_TILE, T)
    return pl.pallas_call(
        _route_kernel,
        grid=(2, T // tr),
        in_specs=[pl.BlockSpec((tr, LANES), lambda ph, t: (t, 0))],
        out_specs=[pl.BlockSpec((tr, TOP_K), lambda ph, t: (ph * t, 0)),
                   pl.BlockSpec((tr, TOP_K), lambda ph, t: (ph * t, 0)),
                   pl.BlockSpec((1, LANES), lambda ph, t: (0, 0))],
        out_shape=[jax.ShapeDtypeStruct((T, TOP_K), jnp.int32),
                   jax.ShapeDtypeStruct((T, TOP_K), F32),
                   jax.ShapeDtypeStruct((1, LANES), F32)],
        scratch_shapes=[pltpu.VMEM((1, LANES), F32)] * 3,
        compiler_params=_params("arbitrary", "arbitrary"),
        name="route",
    )(logits)


def _row_copy(src, dst, sem):
    return pltpu.make_async_copy(src, dst, sem)


def _dispatch_kernel(dest_ref, h_ref, xs_ref, sem):
    td = h_ref.shape[0]

    def issue(t, _):
        for k in range(TOP_K):
            d = dest_ref[0, 0, t * TOP_K + k]
            _row_copy(h_ref.at[pl.ds(t, 1), :], xs_ref.at[pl.ds(d, 1), :], sem).start()
        return 0

    lax.fori_loop(0, td, issue, 0)

    def drain(t, _):
        for k in range(TOP_K):
            _row_copy(h_ref.at[pl.ds(0, 1), :], xs_ref.at[pl.ds(0, 1), :], sem).wait()
        return 0

    lax.fori_loop(0, td, drain, 0)


def _dispatch(h2, dest):
    T, D = h2.shape
    td = min(ROUTE_TILE, T)
    dest3 = dest.reshape(T // td, 1, td * TOP_K)
    return pl.pallas_call(
        _dispatch_kernel,
        grid=(T // td,),
        in_specs=[pl.BlockSpec((1, 1, td * TOP_K), lambda i: (i, 0, 0), memory_space=pltpu.SMEM),
                  pl.BlockSpec((td, D), lambda i: (i, 0))],
        out_specs=pl.BlockSpec(memory_space=pl.ANY),
        out_shape=jax.ShapeDtypeStruct((T * TOP_K, D), h2.dtype),
        scratch_shapes=[pltpu.SemaphoreType.DMA(())],
        compiler_params=_params("arbitrary"),
        name="dispatch",
    )(dest3, h2)


MOE_FF_CHUNK = 256


def _moe_kernel(blk_ref, exp_ref, lo_ref, hi_ref, first_ref, n_ref,
                x_ref, wgu_ref, bgu_ref, wd_ref, bd_ref, o_ref):
    w = pl.program_id(0)
    bm = x_ref.shape[0]

    @pl.when(w < n_ref[0])
    def _():
        x = x_ref[...].astype(BF16)
        res = jnp.zeros((bm, D_MODEL), F32) + bd_ref[0]
        for n in range(D_FF // MOE_FF_CHUNK):
            c0 = n * MOE_FF_CHUNK
            c1 = c0 + MOE_FF_CHUNK
            g = _dot(x, wgu_ref[0, :, c0:c1]) + bgu_ref[0, :, c0:c1]
            u = _dot(x, wgu_ref[0, :, D_FF + c0:D_FF + c1]) + bgu_ref[0, :, D_FF + c0:D_FF + c1]
            g = jnp.minimum(g, SWIGLU_LIMIT)
            u = jnp.clip(u, -SWIGLU_LIMIT, SWIGLU_LIMIT)
            act = (u + 1.0) * (g * _sigmoid(SWIGLU_ALPHA * g))
            res = res + _dot(act.astype(BF16), wd_ref[0, c0:c1, :])

        @pl.when(first_ref[w] == 1)
        def _():
            o_ref[...] = res

        @pl.when(first_ref[w] == 0)
        def _():
            row = blk_ref[w] * bm + lax.broadcasted_iota(jnp.int32, (bm, 1), 0)
            mine = jnp.logical_and(row >= lo_ref[w], row < hi_ref[w])
            o_ref[...] = jnp.where(mine, res, o_ref[...])


def _moe_schedule(counts, n_rows, bm):
    c = counts.astype(jnp.int32)
    end = jnp.cumsum(c)
    start = end - c
    first_blk = start // bm
    last_blk = jnp.where(c > 0, (end - 1) // bm, first_blk - 1)
    n_items_e = last_blk - first_blk + 1
    item_end = jnp.cumsum(n_items_e)
    item_start = item_end - n_items_e
    n_items = item_end[-1]
    w_max = n_rows // bm + N_EXPERTS - 1
    w = jnp.minimum(jnp.arange(w_max, dtype=jnp.int32), n_items - 1)
    e = jnp.searchsorted(item_end, w, side="right").astype(jnp.int32)
    blk = first_blk[e] + (w - item_start[e])
    prev = jnp.concatenate([jnp.full((1,), -1, jnp.int32), blk[:-1]])
    first = (blk != prev).astype(jnp.int32)
    return blk, e, start[e], end[e], first, n_items.reshape(1)


def _moe(xs, counts, wgu_bf, bgu, wd_bf, bd):
    R, D = xs.shape
    bm = min(MOE_TILE, R)
    blk, e, lo, hi, first, n_items = _moe_schedule(counts, R, bm)
    w_max = blk.shape[0]
    return pl.pallas_call(
        _moe_kernel,
        grid_spec=pltpu.PrefetchScalarGridSpec(
            num_scalar_prefetch=6,
            grid=(w_max,),
            in_specs=[pl.BlockSpec((bm, D), lambda w, blk, e, *_: (blk[w], 0)),
                      pl.BlockSpec((1, D, 2 * D_FF), lambda w, blk, e, *_: (e[w], 0, 0)),
                      pl.BlockSpec((1, 1, 2 * D_FF), lambda w, blk, e, *_: (e[w], 0, 0)),
                      pl.BlockSpec((1, D_FF, D), lambda w, blk, e, *_: (e[w], 0, 0)),
                      pl.BlockSpec((1, 1, D), lambda w, blk, e, *_: (e[w], 0, 0))],
            out_specs=pl.BlockSpec((bm, D), lambda w, blk, e, *_: (blk[w], 0))),
        out_shape=jax.ShapeDtypeStruct((R, D), F32),
        compiler_params=_params("arbitrary"),
        name="moe",
    )(blk, e, lo, hi, first, n_items, xs, wgu_bf,
      bgu.reshape(N_EXPERTS, 1, 2 * D_FF), wd_bf, bd.reshape(N_EXPERTS, 1, D))


def _combine_kernel(dest_ref, gate_ref, x1_ref, ga_ref, g_ref, ys_ref, o_ref, buf, sem):
    tc = x1_ref.shape[1]

    def issue(t, _):
        for k in range(TOP_K):
            d = dest_ref[0, 0, 0, t * TOP_K + k]
            _row_copy(ys_ref.at[pl.ds(d, 1), :], buf.at[k, pl.ds(t, 1), :], sem).start()
        return 0

    lax.fori_loop(0, tc, issue, 0)

    def drain(t, _):
        for k in range(TOP_K):
            _row_copy(ys_ref.at[pl.ds(0, 1), :], buf.at[0, pl.ds(0, 1), :], sem).wait()
        return 0

    lax.fori_loop(0, tc, drain, 0)
    gate = gate_ref[0]
    ff = jnp.zeros((tc, D_MODEL), F32)
    for k in range(TOP_K):
        ff = ff + buf[k] * gate[:, k:k + 1]
    x2 = x1_ref[0] + ga_ref[0] * ff
    ms = jnp.mean(x2 * x2, axis=-1, keepdims=True)
    o_ref[0] = x2 * lax.rsqrt(ms + EPS) * g_ref[...]


def _combine(ys, dest, gates, x1, ga, g_final):
    G, R, D = x1.shape
    tc = min(COMBINE_TILE, R)
    nt = R // tc
    dest4 = dest.reshape(G, nt, 1, tc * TOP_K)
    gates3 = gates.reshape(G, R, TOP_K)
    return pl.pallas_call(
        _combine_kernel,
        grid=(G, nt),
        in_specs=[pl.BlockSpec((1, 1, 1, tc * TOP_K), lambda g, t: (g, t, 0, 0),
                               memory_space=pltpu.SMEM),
                  pl.BlockSpec((1, tc, TOP_K), lambda g, t: (g, t, 0)),
                  pl.BlockSpec((1, tc, D), lambda g, t: (g, t, 0)),
                  _mod_spec(ga, tc),
                  pl.BlockSpec((1, D), lambda g, t: (0, 0)),
                  pl.BlockSpec(memory_space=pl.ANY)],
        out_specs=pl.BlockSpec((1, tc, D), lambda g, t: (g, t, 0)),
        out_shape=jax.ShapeDtypeStruct((G, R, D), F32),
        scratch_shapes=[pltpu.VMEM((TOP_K, tc, D), F32), pltpu.SemaphoreType.DMA(())],
        compiler_params=_params("arbitrary", "arbitrary"),
        name="combine",
    )(dest4, gates3, x1, ga, g_final.reshape(1, D), ys)


def _ffn(x, attn, conv, mods, wts, g_final):
    ga1, sc2, sh2, ga2 = mods
    G, R, D = x.shape
    x1, h2, logits = _outproj(x, attn, conv, ga1, sc2, sh2, wts["g_ffn"], wts["w_out"],
                              wts["wr_hi"], wts["wr_lo"], wts["br"])
    T = G * R
    dest, gates, counts = _route(logits.reshape(T, LANES))
    xs = _dispatch(h2.reshape(T, D), dest)
    ys = _moe(xs, counts[0, :N_EXPERTS], wts["wgu"], wts["bgu"], wts["wd"], wts["bd"])
    return _combine(ys, dest, gates, x1, ga2, g_final)


def kernel(x_prompt, x_sample, c_prompt, c_sample, cache_k, cache_v, state_conv, page_table,
           g_attn_norm, g_ffn_norm, w_ada, b_ada, w_in, sb_bias, w_dw, b_dw, ln_conv_g, ln_conv_b,
           w_out, w_router, b_router, w_gate_up, b_gate_up, w_down, b_down, g_final):
    assert w_ada.shape[0] == 1, "one trunk layer"
    B, S, D = x_prompt.shape
    Bs, nq, _ = x_sample.shape
    n_pool = cache_k.shape[1]

    wr = jnp.pad(w_router[0], ((0, 0), (0, LANES - N_EXPERTS)))
    wr_hi = wr.astype(BF16)
    wts = dict(
        g_ffn=g_ffn_norm[0], w_out=w_out[0].astype(BF16),
        wr_hi=wr_hi, wr_lo=(wr - wr_hi.astype(F32)).astype(BF16),
        br=jnp.pad(b_router[0], (0, LANES - N_EXPERTS), constant_values=NEG_BIG).reshape(1, LANES),
        wgu=w_gate_up[0].astype(BF16), bgu=b_gate_up[0],
        wd=w_down[0].astype(BF16), bd=b_down[0])
    w_in_bf = w_in[0].astype(BF16)

    ada = _ada(jnp.concatenate([c_prompt, c_sample], axis=0), w_ada[0], b_ada[0])
    ada_p = ada[:B].reshape(B, 1, 6, D)
    ada_s = jnp.broadcast_to(ada[B:].reshape(Bs, 1, 6, D), (Bs, nq, 6, D)).reshape(1, Bs * nq, 6, D)
    mod_p = [ada_p[:, :, i, :] for i in range(6)]
    mod_s = [ada_s[:, :, i, :] for i in range(6)]

    q, k, v, kb, vb, glu = _inproj(x_prompt, mod_p[1], mod_p[0], g_attn_norm[0], w_in_bf)
    attn = _attn_prompt(q, kb, vb, sb_bias[0])
    conv = _conv_prompt(glu, w_dw[0], b_dw[0], ln_conv_g[0], ln_conv_b[0])
    y_prompt = _ffn(x_prompt, attn, conv, (mod_p[2], mod_p[4], mod_p[3], mod_p[5]), wts, g_final)
    k_prompt = k.reshape(1, B, S, N_HEADS, HEAD_DIM)
    v_prompt = v.reshape(1, B, S, N_HEADS, HEAD_DIM)
    conv_prompt = glu[:, S - CONV_HIST:, :][None]

    xs = x_sample.reshape(1, Bs * nq, D)
    q_s, k_s, v_s, kb_s, vb_s, glu_s = _inproj(xs, mod_s[1], mod_s[0], g_attn_norm[0], w_in_bf)

    def new_page(a):
        a = jnp.transpose(a.reshape(Bs, nq, D_ATTN), (0, 2, 1))
        return jnp.pad(a, ((0, 0), (0, 0), (0, PAGE_SIZE - nq)))

    to_pages = lambda c: jnp.transpose(c[0], (0, 2, 3, 1))
    bias_rows = jnp.tile(sb_bias[0], nq).reshape(nq * N_HEADS, 1)
    attn_s = _attn_sample(q_s.reshape(Bs, nq, D_ATTN).astype(F32), new_page(kb_s), new_page(vb_s), bias_rows,
                          to_pages(cache_k), to_pages(cache_v), page_table)
    state_t = jnp.transpose(state_conv[0], (1, 0, 2))
    glu_t = jnp.transpose(glu_s.reshape(Bs, nq, CONV_CH), (1, 0, 2))
    conv_t, nstate_t = _conv_sample(state_t, glu_t, w_dw[0], b_dw[0], ln_conv_g[0], ln_conv_b[0])
    conv_s = jnp.transpose(conv_t, (1, 0, 2)).reshape(1, Bs * nq, CONV_CH)
    y_s = _ffn(xs, attn_s.reshape(1, Bs * nq, D_ATTN).astype(BF16), conv_s,
               (mod_s[2], mod_s[4], mod_s[3], mod_s[5]), wts, g_final)
    y_sample = y_s.reshape(Bs, nq, D)
    k_sample = k_s.reshape(1, Bs, nq, N_HEADS, HEAD_DIM)
    v_sample = v_s.reshape(1, Bs, nq, N_HEADS, HEAD_DIM)
    conv_sample = jnp.transpose(nstate_t, (1, 0, 2))[None]

    return (y_prompt, y_sample, k_prompt, v_prompt, conv_prompt, k_sample, v_sample, conv_sample)
```

```python
import functools

import jax
import jax.numpy as jnp
from jax import lax
from jax.experimental import pallas as pl
from jax.experimental.pallas import tpu as pltpu

F32 = jnp.float32
BF16 = jnp.bfloat16

D_MODEL = 1024
N_HEADS = 8
HEAD_DIM = 64
D_ATTN = N_HEADS * HEAD_DIM
CONV_CH = D_MODEL - D_ATTN
CONV_WIDTH = 31
CONV_HIST = CONV_WIDTH - 1
D_IN = 3 * D_ATTN + 2 * CONV_CH
N_EXPERTS = 32
TOP_K = 4
D_FF = D_MODEL
SWIGLU_LIMIT = 7.0
SWIGLU_ALPHA = 1.702
EPS = 1e-5
PAGE_SIZE = 128

LANES = 128
SUBLANES = 8
V7X_VMEM_BYTES = 64 * 1024 * 1024
VMEM_LIMIT = 48 * 1024 * 1024

ROW_TILE = 512
ATTN_TILE = 256
CONV_TILE = 256
ROUTE_TILE = 512
MOE_TILE = 512
COMBINE_TILE = 256
PAGES_PER_STEP = 8
NEG_BIG = -1e30


def _params(*sem):
    return pltpu.CompilerParams(dimension_semantics=sem, vmem_limit_bytes=VMEM_LIMIT)


def _sigmoid(x):
    return 1.0 / (1.0 + jnp.exp(-x))


def _log_sigmoid_neg(z):
    return -(jnp.maximum(z, 0.0) + jnp.log(1.0 + jnp.exp(-jnp.abs(z))))


def _split_bf16(x):
    hi = x.astype(BF16)
    lo = (x - hi.astype(F32)).astype(BF16)
    return hi, lo


def _upper_incl(n):
    r = lax.broadcasted_iota(jnp.int32, (n, n), 0)
    c = lax.broadcasted_iota(jnp.int32, (n, n), 1)
    return jnp.where(r >= c, 1.0, 0.0).astype(BF16)


def _dot(a, b):
    return jnp.dot(a, b, preferred_element_type=F32)


def _dot_t(a, b):
    return lax.dot_general(a, b, (((1,), (1,)), ((), ())), preferred_element_type=F32)


def _ada_kernel(c_ref, w_ref, b_ref, o_ref):
    c = c_ref[...]
    s = (c * _sigmoid(c)).astype(BF16)
    o_ref[...] = _dot(s, w_ref[...].astype(BF16)) + b_ref[...]


def _ada(c_all, w_ada, b_ada):
    n, d = c_all.shape
    nout = w_ada.shape[1]
    tn = 1024
    return pl.pallas_call(
        _ada_kernel,
        grid=(nout // tn,),
        in_specs=[pl.BlockSpec((n, d), lambda j: (0, 0)),
                  pl.BlockSpec((d, tn), lambda j: (0, j)),
                  pl.BlockSpec((1, tn), lambda j: (0, j))],
        out_specs=pl.BlockSpec((n, tn), lambda j: (0, j)),
        out_shape=jax.ShapeDtypeStruct((n, nout), F32),
        compiler_params=_params("arbitrary"),
        name="ada",
    )(c_all, w_ada, b_ada.reshape(1, nout))


def _inproj_kernel(x_ref, sc_ref, sh_ref, g_ref, w_ref,
                   q_ref, k_ref, v_ref, kb_ref, vb_ref, glu_ref):
    x = x_ref[0]
    ms = jnp.mean(x * x, axis=-1, keepdims=True)
    h = x * lax.rsqrt(ms + EPS) * g_ref[...]
    h = (h * (1.0 + sc_ref[0]) + sh_ref[0]).astype(BF16)
    c = D_ATTN
    q_ref[0] = (_dot(h, w_ref[:, 0:c]) * (HEAD_DIM ** -0.5)).astype(BF16)
    k = _dot(h, w_ref[:, c:2 * c])
    k_ref[0] = k
    kb_ref[0] = k.astype(BF16)
    v = _dot(h, w_ref[:, 2 * c:3 * c])
    v_ref[0] = v
    vb_ref[0] = v.astype(BF16)
    a = _dot(h, w_ref[:, 3 * c:3 * c + CONV_CH])
    g = _dot(h, w_ref[:, 3 * c + CONV_CH:])
    glu_ref[0] = a * _sigmoid(g)


def _mod_spec(mod, ts):
    if mod.shape[1] == 1:
        return pl.BlockSpec((1, 1, D_MODEL), lambda g, t: (g, 0, 0))
    return pl.BlockSpec((1, ts, D_MODEL), lambda g, t: (g, t, 0))


def _inproj(x, sc, sh, g_norm, w_in_bf):
    G, R, D = x.shape
    ts = min(ROW_TILE, R)
    tok = lambda n: pl.BlockSpec((1, ts, n), lambda g, t: (g, t, 0))
    shp = lambda n, dt: jax.ShapeDtypeStruct((G, R, n), dt)
    return pl.pallas_call(
        _inproj_kernel,
        grid=(G, R // ts),
        in_specs=[tok(D), _mod_spec(sc, ts), _mod_spec(sh, ts),
                  pl.BlockSpec((1, D), lambda g, t: (0, 0)),
                  pl.BlockSpec((D, D_IN), lambda g, t: (0, 0))],
        out_specs=[tok(D_ATTN)] * 5 + [tok(CONV_CH)],
        out_shape=[shp(D_ATTN, BF16), shp(D_ATTN, F32), shp(D_ATTN, F32),
                   shp(D_ATTN, BF16), shp(D_ATTN, BF16), shp(CONV_CH, F32)],
        compiler_params=_params("arbitrary", "arbitrary"),
        name="inproj",
    )(x, sc, sh, g_norm.reshape(1, D), w_in_bf)


def _sb_scores(z, upper_incl, valid):
    m = _log_sigmoid_neg(z)
    if valid is not None:
        m = jnp.where(valid, m, 0.0)
    m_hi, m_lo = _split_bf16(m)
    expo = z + (_dot(m_hi, upper_incl) + _dot(m_lo, upper_incl))
    return expo, jnp.sum(m, axis=-1, keepdims=True)


def _sb_weights(expo, carry, valid):
    a = jnp.exp(expo + carry)
    if valid is not None:
        a = jnp.where(valid, a, 0.0)
    return a.astype(BF16)


def _attn_prompt_kernel(bias_ref, q_ref, k_ref, v_ref, o_ref, acc_e, acc_o, car_e, car_o):
    t = ATTN_TILE
    p = pl.program_id(1)
    qi = pl.program_id(2)
    q2 = q_ref[0]
    lane = lax.broadcasted_iota(jnp.int32, (1, LANES), 1)
    even = lane < HEAD_DIM
    q_e = jnp.where(even, q2, jnp.zeros_like(q2))
    q_o = jnp.where(even, jnp.zeros_like(q2), q2)
    b_e = bias_ref[2 * p]
    b_o = bias_ref[2 * p + 1]
    upper = _upper_incl(t)

    def tiles(kbs, valid):
        ce, co = car_e[...], car_o[...]
        oe = oo = None
        for kb in kbs:
            st = pl.multiple_of(kb * t, t)
            kblk = k_ref[0, pl.ds(st, t), :]
            vblk = v_ref[0, pl.ds(st, t), :]
            xe, se = _sb_scores(_dot_t(q_e, kblk) + b_e, upper, valid)
            xo, so = _sb_scores(_dot_t(q_o, kblk) + b_o, upper, valid)
            de = _dot(_sb_weights(xe, ce, valid), vblk)
            do = _dot(_sb_weights(xo, co, valid), vblk)
            oe = de if oe is None else oe + de
            oo = do if oo is None else oo + do
            ce = ce + se
            co = co + so
        acc_e[...] += oe
        acc_o[...] += oo
        car_e[...] = ce
        car_o[...] = co

    acc_e[...] = jnp.zeros_like(acc_e)
    acc_o[...] = jnp.zeros_like(acc_o)
    car_e[...] = jnp.zeros_like(car_e)
    car_o[...] = jnp.zeros_like(car_o)

    r = lax.broadcasted_iota(jnp.int32, (t, t), 0)
    c = lax.broadcasted_iota(jnp.int32, (t, t), 1)
    tiles([qi], c < r)

    def body(i, _):
        kb = qi - 1 - 2 * i
        tiles([kb, kb - 1], None)
        return 0

    lax.fori_loop(0, qi // 2, body, 0)

    @pl.when(qi % 2 == 1)
    def _():
        tiles([0], None)

    o_ref[0] = jnp.where(even, acc_e[...], acc_o[...]).astype(o_ref.dtype)


def _attn_prompt(q, kb, vb, sb_bias):
    B, S, _ = q.shape
    t = ATTN_TILE
    npair = N_HEADS // 2
    return pl.pallas_call(
        _attn_prompt_kernel,
        grid_spec=pltpu.PrefetchScalarGridSpec(
            num_scalar_prefetch=0,
            grid=(B, npair, S // t),
            in_specs=[pl.BlockSpec(memory_space=pltpu.SMEM),
                      pl.BlockSpec((1, t, LANES), lambda b, p, i: (b, i, p)),
                      pl.BlockSpec((1, S, LANES), lambda b, p, i: (b, 0, p)),
                      pl.BlockSpec((1, S, LANES), lambda b, p, i: (b, 0, p))],
            out_specs=pl.BlockSpec((1, t, LANES), lambda b, p, i: (b, i, p)),
            scratch_shapes=[pltpu.VMEM((t, LANES), F32), pltpu.VMEM((t, LANES), F32),
                            pltpu.VMEM((t, 1), F32), pltpu.VMEM((t, 1), F32)]),
        out_shape=jax.ShapeDtypeStruct((B, S, D_ATTN), BF16),
        compiler_params=_params("arbitrary", "arbitrary", "arbitrary"),
        name="attn_prompt",
    )(sb_bias, q, kb, vb)


CONV_PAD = 32
CONV_TIME = 64


def _conv_prompt_kernel(glu_ref, w_ref, b_ref, g_ref, beta_ref, o_ref, ext_ref, y_ref, sh_ref):
    S = glu_ref.shape[1]
    tt = CONV_TIME
    ext_ref[0:CONV_PAD, :] = jnp.zeros((CONV_PAD, CONV_CH), F32)
    ext_ref[CONV_PAD:CONV_PAD + S, :] = glu_ref[0]
    off = CONV_PAD - CONV_HIST

    def conv_tile(i, _):
        t0 = pl.multiple_of(i * tt, tt)
        for cb in range(CONV_CH // LANES):
            cs = slice(cb * LANES, (cb + 1) * LANES)
            win = ext_ref[pl.ds(t0, tt + CONV_PAD), cs]
            acc = jnp.zeros((tt, LANES), F32) + b_ref[:, cs]
            for res in range(SUBLANES):
                taps = [w for w in range(CONV_WIDTH) if (off + w) % SUBLANES == res]
                span = max(off + w - res for w in taps) + tt
                sh_ref[res, 0:span, :] = win[res:res + span, :]
                for w in taps:
                    base = off + w - res
                    acc = acc + sh_ref[res, base:base + tt, :] * w_ref[w:w + 1, cs]
            y_ref[pl.ds(t0, tt), cs] = acc
        return 0

    lax.fori_loop(0, S // tt, conv_tile, 0)

    def norm_tile(i, _):
        t0 = pl.multiple_of(i * tt, tt)
        acc = y_ref[pl.ds(t0, tt), :]
        mu = jnp.mean(acc, axis=-1, keepdims=True)
        d = acc - mu
        var = jnp.mean(d * d, axis=-1, keepdims=True)
        y = d * lax.rsqrt(var + EPS) * g_ref[...] + beta_ref[...]
        o_ref[0, pl.ds(t0, tt), :] = (y * _sigmoid(y)).astype(o_ref.dtype)
        return 0

    lax.fori_loop(0, S // tt, norm_tile, 0)


def _conv_prompt(glu, w_dw, b_dw, ln_g, ln_b):
    B, S, C = glu.shape
    row = lambda: pl.BlockSpec((1, C), lambda b: (0, 0))
    return pl.pallas_call(
        _conv_prompt_kernel,
        grid=(B,),
        in_specs=[pl.BlockSpec((1, S, C), lambda b: (b, 0, 0)),
                  pl.BlockSpec((CONV_WIDTH, C), lambda b: (0, 0)),
                  row(), row(), row()],
        out_specs=pl.BlockSpec((1, S, C), lambda b: (b, 0, 0)),
        out_shape=jax.ShapeDtypeStruct((B, S, C), BF16),
        scratch_shapes=[pltpu.VMEM((S + CONV_PAD, C), F32), pltpu.VMEM((S, C), F32),
                        pltpu.VMEM((SUBLANES, CONV_TIME + CONV_PAD, LANES), F32)],
        compiler_params=_params("arbitrary"),
        name="conv_prompt",
    )(glu, w_dw, b_dw.reshape(1, C), ln_g.reshape(1, C), ln_b.reshape(1, C))


def _conv_sample_kernel(st_ref, glu_ref, w_ref, b_ref, g_ref, beta_ref, o_ref, nst_ref):
    nq = glu_ref.shape[0]

    def ext(j):
        return st_ref[j] if j < CONV_HIST else glu_ref[j - CONV_HIST]

    for i in range(nq):
        acc = jnp.zeros(st_ref.shape[1:], F32) + b_ref[...]
        for w in range(CONV_WIDTH):
            acc = acc + ext(i + w) * w_ref[w:w + 1, :]
        mu = jnp.mean(acc, axis=-1, keepdims=True)
        d = acc - mu
        var = jnp.mean(d * d, axis=-1, keepdims=True)
        y = d * lax.rsqrt(var + EPS) * g_ref[...] + beta_ref[...]
        o_ref[i] = (y * _sigmoid(y)).astype(o_ref.dtype)
    for j in range(CONV_HIST):
        nst_ref[j] = ext(j + nq)


def _conv_sample(state_t, glu_t, w_dw, b_dw, ln_g, ln_b):
    H, Bs, C = state_t.shape
    nq = glu_t.shape[0]
    full = lambda shape: pl.BlockSpec(shape, lambda i: (0,) * len(shape))
    return pl.pallas_call(
        _conv_sample_kernel,
        grid=(1,),
        in_specs=[full((H, Bs, C)), full((nq, Bs, C)), full((CONV_WIDTH, C)),
                  full((1, C)), full((1, C)), full((1, C))],
        out_specs=[full((nq, Bs, C)), full((H, Bs, C))],
        out_shape=[jax.ShapeDtypeStruct((nq, Bs, C), BF16),
                   jax.ShapeDtypeStruct((H, Bs, C), F32)],
        compiler_params=_params("arbitrary"),
        name="conv_sample",
    )(state_t, glu_t, w_dw, b_dw.reshape(1, C), ln_g.reshape(1, C), ln_b.reshape(1, C))


def _attn_sample_kernel(pt_ref, q_ref, kn_ref, vn_ref, bias_ref, *rest):
    G = PAGES_PER_STEP
    k_refs = rest[:G]
    v_refs = rest[G:2 * G]
    o_ref = rest[2 * G]
    acc_ref, car_ref = rest[2 * G + 1:]
    s = pl.program_id(1)
    nq = q_ref.shape[1]
    rows = nq * N_HEADS

    hrow = lax.broadcasted_iota(jnp.int32, (N_HEADS, D_ATTN), 0)
    hcol = lax.broadcasted_iota(jnp.int32, (N_HEADS, D_ATTN), 1) // HEAD_DIM
    headmask = hrow == hcol
    q = q_ref[0]
    qbd = jnp.concatenate(
        [jnp.where(headmask, q[i:i + 1, :], 0.0) for i in range(nq)], axis=0).astype(BF16)
    bias = bias_ref[...]
    upper = _upper_incl(PAGE_SIZE)

    @pl.when(s == 0)
    def _():
        r = lax.broadcasted_iota(jnp.int32, (rows, PAGE_SIZE), 0) // N_HEADS
        c = lax.broadcasted_iota(jnp.int32, (rows, PAGE_SIZE), 1)
        valid = c < r
        expo, rowsum = _sb_scores(_dot(qbd, kn_ref[0]) + bias, upper, valid)
        acc_ref[...] = _dot_t(_sb_weights(expo, 0.0, valid), vn_ref[0])
        car_ref[...] = rowsum

    car = car_ref[...]
    acc = acc_ref[...]
    kts = [k_refs[j][0].reshape(D_ATTN, PAGE_SIZE).astype(BF16) for j in range(G)]
    zs = [_dot(qbd, kt) + bias for kt in kts]
    scores = [_sb_scores(z, upper, None) for z in zs]
    for j, (expo, rowsum) in enumerate(scores):
        vt = v_refs[j][0].reshape(D_ATTN, PAGE_SIZE).astype(BF16)
        acc = acc + _dot_t(_sb_weights(expo, car, None), vt)
        car = car + rowsum
    acc_ref[...] = acc
    car_ref[...] = car

    @pl.when(s == pl.num_programs(1) - 1)
    def _():
        acc = acc_ref[...]
        out = [jnp.sum(jnp.where(headmask, acc[i * N_HEADS:(i + 1) * N_HEADS, :], 0.0),
                       axis=0, keepdims=True) for i in range(nq)]
        o_ref[0] = jnp.concatenate(out, axis=0).astype(o_ref.dtype)


def _attn_sample(q, knew_t, vnew_t, bias_rows, cache_kt, cache_vt, page_table):
    Bs, nq, _ = q.shape
    n_pages = page_table.shape[1]
    G = PAGES_PER_STEP
    assert n_pages % G == 0
    rows = nq * N_HEADS

    def page_spec(j):
        return pl.BlockSpec(
            (1, N_HEADS, HEAD_DIM, PAGE_SIZE),
            lambda b, s, pt, j=j: (pt[b, n_pages - 1 - (s * G + j)], 0, 0, 0))

    per_b = lambda shape: pl.BlockSpec((1,) + shape, lambda b, s, pt: (b, 0, 0))
    return pl.pallas_call(
        _attn_sample_kernel,
        grid_spec=pltpu.PrefetchScalarGridSpec(
            num_scalar_prefetch=1,
            grid=(Bs, n_pages // G),
            in_specs=[per_b((nq, D_ATTN)), per_b((D_ATTN, PAGE_SIZE)), per_b((D_ATTN, PAGE_SIZE)),
                      pl.BlockSpec((rows, 1), lambda b, s, pt: (0, 0))]
                     + [page_spec(j) for j in range(G)] * 2,
            out_specs=per_b((nq, D_ATTN)),
            scratch_shapes=[pltpu.VMEM((rows, D_ATTN), F32), pltpu.VMEM((rows, 1), F32)]),
        out_shape=jax.ShapeDtypeStruct((Bs, nq, D_ATTN), F32),
        compiler_params=_params("arbitrary", "arbitrary"),
        name="attn_sample",
    )(page_table, q, knew_t, vnew_t, bias_rows, *([cache_kt] * G), *([cache_vt] * G))


def _outproj_kernel(x_ref, a_ref, c_ref, ga_ref, sc_ref, sh_ref, g_ref, w_ref,
                    wr_hi_ref, wr_lo_ref, br_ref, x1_ref, h2_ref, lg_ref):
    mix = _dot(a_ref[0], w_ref[0:D_ATTN, :]) + _dot(c_ref[0], w_ref[D_ATTN:, :])
    x1 = x_ref[0] + ga_ref[0] * mix
    x1_ref[0] = x1
    ms = jnp.mean(x1 * x1, axis=-1, keepdims=True)
    h2 = x1 * lax.rsqrt(ms + EPS) * g_ref[...]
    h2 = h2 * (1.0 + sc_ref[0]) + sh_ref[0]
    h2_ref[0] = h2
    hi, lo = _split_bf16(h2)
    lg_ref[0] = (_dot(hi, wr_hi_ref[...]) + _dot(lo, wr_hi_ref[...])
                 + _dot(hi, wr_lo_ref[...]) + br_ref[...])


def _outproj(x, attn, conv, ga, sc, sh, g_norm, w_out_bf, wr_hi, wr_lo, br_pad):
    G, R, D = x.shape
    ts = min(ROW_TILE, R)
    tok = lambda n: pl.BlockSpec((1, ts, n), lambda g, t: (g, t, 0))
    const = lambda shape: pl.BlockSpec(shape, lambda g, t: (0, 0))
    return pl.pallas_call(
        _outproj_kernel,
        grid=(G, R // ts),
        in_specs=[tok(D), tok(D_ATTN), tok(CONV_CH), _mod_spec(ga, ts), _mod_spec(sc, ts),
                  _mod_spec(sh, ts), const((1, D)), const((D, D)),
                  const((D, LANES)), const((D, LANES)), const((1, LANES))],
        out_specs=[tok(D), tok(D), tok(LANES)],
        out_shape=[jax.ShapeDtypeStruct((G, R, D), F32), jax.ShapeDtypeStruct((G, R, D), F32),
                   jax.ShapeDtypeStruct((G, R, LANES), F32)],
        compiler_params=_params("arbitrary", "arbitrary"),
        name="outproj",
    )(x, attn, conv, ga, sc, sh, g_norm.reshape(1, D), w_out_bf, wr_hi, wr_lo, br_pad)


def _route_kernel(lg_ref, dest_ref, gate_ref, cnt_ref, counts, offs, run):
    ph = pl.program_id(0)
    t = pl.program_id(1)
    tr = lg_ref.shape[0]
    lane = lax.broadcasted_iota(jnp.int32, (tr, LANES), 1).astype(F32)
    l = lg_ref[...]
    onehots, vals = [], []
    for _ in range(TOP_K):
        mx = jnp.max(l, axis=-1, keepdims=True)
        idx = jnp.min(jnp.where(l == mx, lane, float(LANES)), axis=-1, keepdims=True)
        oh = lane == idx
        onehots.append(oh)
        vals.append(mx)
        l = jnp.where(oh, NEG_BIG, l)
    total = jnp.zeros((tr, LANES), F32)
    for oh in onehots:
        total = total + jnp.where(oh, 1.0, 0.0)
    colsum = jnp.sum(total, axis=0, keepdims=True)

    @pl.when(jnp.logical_and(ph == 0, t == 0))
    def _():
        counts[...] = jnp.zeros_like(counts)

    @pl.when(ph == 0)
    def _():
        counts[...] += colsum

    @pl.when(jnp.logical_and(ph == 1, t == 0))
    def _():
        c = jnp.broadcast_to(counts[...], (8, LANES))
        c1 = c.astype(BF16)
        r1 = c - c1.astype(F32)
        c2 = r1.astype(BF16)
        c3 = (r1 - c2.astype(F32)).astype(BF16)
        rr = lax.broadcasted_iota(jnp.int32, (LANES, LANES), 0)
        cc = lax.broadcasted_iota(jnp.int32, (LANES, LANES), 1)
        before = jnp.where(rr < cc, 1.0, 0.0).astype(BF16)
        o = _dot(c1, before) + _dot(c2, before) + _dot(c3, before)
        offs[...] = o[0:1, :]
        run[...] = jnp.zeros_like(run)
        cnt_ref[...] = counts[...]

    @pl.when(ph == 1)
    def _():
        rr = lax.broadcasted_iota(jnp.int32, (tr, tr), 0)
        cc = lax.broadcasted_iota(jnp.int32, (tr, tr), 1)
        earlier = jnp.where(cc < rr, 1.0, 0.0).astype(BF16)
        pos = _dot(earlier, total.astype(BF16)) + (offs[...] + run[...])
        kcol = lax.broadcasted_iota(jnp.int32, (tr, LANES), 1)
        dest = jnp.zeros((tr, LANES), F32)
        gate = jnp.zeros((tr, LANES), F32)
        es = [jnp.exp(v - vals[0]) for v in vals]
        den = es[0] + es[1] + es[2] + es[3]
        for k in range(TOP_K):
            dk = jnp.sum(jnp.where(onehots[k], pos, 0.0), axis=-1, keepdims=True)
            dest = jnp.where(kcol == k, dk, dest)
            gate = jnp.where(kcol == k, es[k] / den, gate)
        dest_ref[...] = dest[:, 0:TOP_K].astype(jnp.int32)
        gate_ref[...] = gate[:, 0:TOP_K]
        run[...] += colsum


def _route(logits):
    T = logits.shape[0]
    tr = min(ROUTE_TILE, T)
    return pl.pallas_call(
        _route_kernel,
        grid=(2, T // tr),
        in_specs=[pl.BlockSpec((tr, LANES), lambda ph, t: (t, 0))],
        out_specs=[pl.BlockSpec((tr, TOP_K), lambda ph, t: (ph * t, 0)),
                   pl.BlockSpec((tr, TOP_K), lambda ph, t: (ph * t, 0)),
                   pl.BlockSpec((1, LANES), lambda ph, t: (0, 0))],
        out_shape=[jax.ShapeDtypeStruct((T, TOP_K), jnp.int32),
                   jax.ShapeDtypeStruct((T, TOP_K), F32),
                   jax.ShapeDtypeStruct((1, LANES), F32)],
        scratch_shapes=[pltpu.VMEM((1, LANES), F32)] * 3,
        compiler_params=_params("arbitrary", "arbitrary"),
        name="route",
    )(logits)


def _row_copy(src, dst, sem):
    return pltpu.make_async_copy(src, dst, sem)


def _dispatch_kernel(dest_ref, h_ref, xs_ref, sem):
    td = h_ref.shape[0]

    def issue(t, _):
        for k in range(TOP_K):
            d = dest_ref[0, 0, t * TOP_K + k]
            _row_copy(h_ref.at[pl.ds(t, 1), :], xs_ref.at[pl.ds(d, 1), :], sem).start()
        return 0

    lax.fori_loop(0, td, issue, 0)

    def drain(t, _):
        for k in range(TOP_K):
            _row_copy(h_ref.at[pl.ds(0, 1), :], xs_ref.at[pl.ds(0, 1), :], sem).wait()
        return 0

    lax.fori_loop(0, td, drain, 0)


def _dispatch(h2, dest):
    T, D = h2.shape
    td = min(ROUTE_TILE, T)
    dest3 = dest.reshape(T // td, 1, td * TOP_K)
    return pl.pallas_call(
        _dispatch_kernel,
        grid=(T // td,),
        in_specs=[pl.BlockSpec((1, 1, td * TOP_K), lambda i: (i, 0, 0), memory_space=pltpu.SMEM),
                  pl.BlockSpec((td, D), lambda i: (i, 0))],
        out_specs=pl.BlockSpec(memory_space=pl.ANY),
        out_shape=jax.ShapeDtypeStruct((T * TOP_K, D), h2.dtype),
        scratch_shapes=[pltpu.SemaphoreType.DMA(())],
        compiler_params=_params("arbitrary"),
        name="dispatch",
    )(dest3, h2)


MOE_FF_CHUNK = 256


def _moe_kernel(blk_ref, exp_ref, lo_ref, hi_ref, first_ref, n_ref,
                x_ref, wgu_ref, bgu_ref, wd_ref, bd_ref, o_ref):
    w = pl.program_id(0)
    bm = x_ref.shape[0]

    @pl.when(w < n_ref[0])
    def _():
        x = x_ref[...].astype(BF16)
        res = jnp.zeros((bm, D_MODEL), F32) + bd_ref[0]
        for n in range(D_FF // MOE_FF_CHUNK):
            c0 = n * MOE_FF_CHUNK
            c1 = c0 + MOE_FF_CHUNK
            g = _dot(x, wgu_ref[0, :, c0:c1]) + bgu_ref[0, :, c0:c1]
            u = _dot(x, wgu_ref[0, :, D_FF + c0:D_FF + c1]) + bgu_ref[0, :, D_FF + c0:D_FF + c1]
            g = jnp.minimum(g, SWIGLU_LIMIT)
            u = jnp.clip(u, -SWIGLU_LIMIT, SWIGLU_LIMIT)
            act = (u + 1.0) * (g * _sigmoid(SWIGLU_ALPHA * g))
            res = res + _dot(act.astype(BF16), wd_ref[0, c0:c1, :])

        @pl.when(first_ref[w] == 1)
        def _():
            o_ref[...] = res

        @pl.when(first_ref[w] == 0)
        def _():
            row = blk_ref[w] * bm + lax.broadcasted_iota(jnp.int32, (bm, 1), 0)
            mine = jnp.logical_and(row >= lo_ref[w], row < hi_ref[w])
            o_ref[...] = jnp.where(mine, res, o_ref[...])


def _moe_schedule(counts, n_rows, bm):
    c = counts.astype(jnp.int32)
    end = jnp.cumsum(c)
    start = end - c
    first_blk = start // bm
    last_blk = jnp.where(c > 0, (end - 1) // bm, first_blk - 1)
    n_items_e = last_blk - first_blk + 1
    item_end = jnp.cumsum(n_items_e)
    item_start = item_end - n_items_e
    n_items = item_end[-1]
    w_max = n_rows // bm + N_EXPERTS - 1
    w = jnp.minimum(jnp.arange(w_max, dtype=jnp.int32), n_items - 1)
    e = jnp.sum((item_end[None, :] <= w[:, None]).astype(jnp.int32), axis=1)
    blk = first_blk[e] + (w - item_start[e])
    prev = jnp.concatenate([jnp.full((1,), -1, jnp.int32), blk[:-1]])
    first = (blk != prev).astype(jnp.int32)
    return blk, e, start[e], end[e], first, n_items.reshape(1)


def _moe(xs, counts, wgu_bf, bgu, wd_bf, bd):
    R, D = xs.shape
    bm = min(MOE_TILE, R)
    blk, e, lo, hi, first, n_items = _moe_schedule(counts, R, bm)
    w_max = blk.shape[0]
    return pl.pallas_call(
        _moe_kernel,
        grid_spec=pltpu.PrefetchScalarGridSpec(
            num_scalar_prefetch=6,
            grid=(w_max,),
            in_specs=[pl.BlockSpec((bm, D), lambda w, blk, e, *_: (blk[w], 0)),
                      pl.BlockSpec((1, D, 2 * D_FF), lambda w, blk, e, *_: (e[w], 0, 0)),
                      pl.BlockSpec((1, 1, 2 * D_FF), lambda w, blk, e, *_: (e[w], 0, 0)),
                      pl.BlockSpec((1, D_FF, D), lambda w, blk, e, *_: (e[w], 0, 0)),
                      pl.BlockSpec((1, 1, D), lambda w, blk, e, *_: (e[w], 0, 0))],
            out_specs=pl.BlockSpec((bm, D), lambda w, blk, e, *_: (blk[w], 0))),
        out_shape=jax.ShapeDtypeStruct((R, D), F32),
        compiler_params=_params("arbitrary"),
        name="moe",
    )(blk, e, lo, hi, first, n_items, xs, wgu_bf,
      bgu.reshape(N_EXPERTS, 1, 2 * D_FF), wd_bf, bd.reshape(N_EXPERTS, 1, D))


def _combine_kernel(dest_ref, gate_ref, x1_ref, ga_ref, g_ref, ys_ref, o_ref, buf, sem):
    tc = x1_ref.shape[1]

    def issue(t, _):
        for k in range(TOP_K):
            d = dest_ref[0, 0, 0, t * TOP_K + k]
            _row_copy(ys_ref.at[pl.ds(d, 1), :], buf.at[k, pl.ds(t, 1), :], sem).start()
        return 0

    lax.fori_loop(0, tc, issue, 0)

    def drain(t, _):
        for k in range(TOP_K):
            _row_copy(ys_ref.at[pl.ds(0, 1), :], buf.at[0, pl.ds(0, 1), :], sem).wait()
        return 0

    lax.fori_loop(0, tc, drain, 0)
    gate = gate_ref[0]
    ff = jnp.zeros((tc, D_MODEL), F32)
    for k in range(TOP_K):
        ff = ff + buf[k] * gate[:, k:k + 1]
    x2 = x1_ref[0] + ga_ref[0] * ff
    ms = jnp.mean(x2 * x2, axis=-1, keepdims=True)
    o_ref[0] = x2 * lax.rsqrt(ms + EPS) * g_ref[...]


def _combine(ys, dest, gates, x1, ga, g_final):
    G, R, D = x1.shape
    tc = min(COMBINE_TILE, R)
    nt = R // tc
    dest4 = dest.reshape(G, nt, 1, tc * TOP_K)
    gates3 = gates.reshape(G, R, TOP_K)
    return pl.pallas_call(
        _combine_kernel,
        grid=(G, nt),
        in_specs=[pl.BlockSpec((1, 1, 1, tc * TOP_K), lambda g, t: (g, t, 0, 0),
                               memory_space=pltpu.SMEM),
                  pl.BlockSpec((1, tc, TOP_K), lambda g, t: (g, t, 0)),
                  pl.BlockSpec((1, tc, D), lambda g, t: (g, t, 0)),
                  _mod_spec(ga, tc),
                  pl.BlockSpec((1, D), lambda g, t: (0, 0)),
                  pl.BlockSpec(memory_space=pl.ANY)],
        out_specs=pl.BlockSpec((1, tc, D), lambda g, t: (g, t, 0)),
        out_shape=jax.ShapeDtypeStruct((G, R, D), F32),
        scratch_shapes=[pltpu.VMEM((TOP_K, tc, D), F32), pltpu.SemaphoreType.DMA(())],
        compiler_params=_params("arbitrary", "arbitrary"),
        name="combine",
    )(dest4, gates3, x1, ga, g_final.reshape(1, D), ys)


def _ffn(x, attn, conv, mods, wts, g_final):
    ga1, sc2, sh2, ga2 = mods
    G, R, D = x.shape
    x1, h2, logits = _outproj(x, attn, conv, ga1, sc2, sh2, wts["g_ffn"], wts["w_out"],
                              wts["wr_hi"], wts["wr_lo"], wts["br"])
    T = G * R
    dest, gates, counts = _route(logits.reshape(T, LANES))
    xs = _dispatch(h2.reshape(T, D), dest)
    ys = _moe(xs, counts[0, :N_EXPERTS], wts["wgu"], wts["bgu"], wts["wd"], wts["bd"])
    return _combine(ys, dest, gates, x1, ga2, g_final)


def kernel(x_prompt, x_sample, c_prompt, c_sample, cache_k, cache_v, state_conv, page_table,
           g_attn_norm, g_ffn_norm, w_ada, b_ada, w_in, sb_bias, w_dw, b_dw, ln_conv_g, ln_conv_b,
           w_out, w_router, b_router, w_gate_up, b_gate_up, w_down, b_down, g_final):
    assert w_ada.shape[0] == 1, "one trunk layer"
    B, S, D = x_prompt.shape
    Bs, nq, _ = x_sample.shape
    n_pool = cache_k.shape[1]

    wr = jnp.pad(w_router[0], ((0, 0), (0, LANES - N_EXPERTS)))
    wr_hi = wr.astype(BF16)
    wts = dict(
        g_ffn=g_ffn_norm[0], w_out=w_out[0].astype(BF16),
        wr_hi=wr_hi, wr_lo=(wr - wr_hi.astype(F32)).astype(BF16),
        br=jnp.pad(b_router[0], (0, LANES - N_EXPERTS), constant_values=NEG_BIG).reshape(1, LANES),
        wgu=w_gate_up[0].astype(BF16), bgu=b_gate_up[0],
        wd=w_down[0].astype(BF16), bd=b_down[0])
    w_in_bf = w_in[0].astype(BF16)

    ada = _ada(jnp.concatenate([c_prompt, c_sample], axis=0), w_ada[0], b_ada[0])
    ada_p = ada[:B].reshape(B, 1, 6, D)
    ada_s = jnp.broadcast_to(ada[B:].reshape(Bs, 1, 6, D), (Bs, nq, 6, D)).reshape(1, Bs * nq, 6, D)
    mod_p = [ada_p[:, :, i, :] for i in range(6)]
    mod_s = [ada_s[:, :, i, :] for i in range(6)]

    q, k, v, kb, vb, glu = _inproj(x_prompt, mod_p[1], mod_p[0], g_attn_norm[0], w_in_bf)
    attn = _attn_prompt(q, kb, vb, sb_bias[0])
    conv = _conv_prompt(glu, w_dw[0], b_dw[0], ln_conv_g[0], ln_conv_b[0])
    y_prompt = _ffn(x_prompt, attn, conv, (mod_p[2], mod_p[4], mod_p[3], mod_p[5]), wts, g_final)
    k_prompt = k.reshape(1, B, S, N_HEADS, HEAD_DIM)
    v_prompt = v.reshape(1, B, S, N_HEADS, HEAD_DIM)
    conv_prompt = glu[:, S - CONV_HIST:, :][None]

    xs = x_sample.reshape(1, Bs * nq, D)
    q_s, k_s, v_s, kb_s, vb_s, glu_s = _inproj(xs, mod_s[1], mod_s[0], g_attn_norm[0], w_in_bf)

    def new_page(a):
        a = jnp.transpose(a.reshape(Bs, nq, D_ATTN), (0, 2, 1))
        return jnp.pad(a, ((0, 0), (0, 0), (0, PAGE_SIZE - nq)))

    to_pages = lambda c: jnp.transpose(c[0], (0, 2, 3, 1))
    bias_rows = jnp.tile(sb_bias[0], nq).reshape(nq * N_HEADS, 1)
    attn_s = _attn_sample(q_s.reshape(Bs, nq, D_ATTN).astype(F32), new_page(kb_s), new_page(vb_s), bias_rows,
                          to_pages(cache_k), to_pages(cache_v), page_table)
    state_t = jnp.transpose(state_conv[0], (1, 0, 2))
    glu_t = jnp.transpose(glu_s.reshape(Bs, nq, CONV_CH), (1, 0, 2))
    conv_t, nstate_t = _conv_sample(state_t, glu_t, w_dw[0], b_dw[0], ln_conv_g[0], ln_conv_b[0])
    conv_s = jnp.transpose(conv_t, (1, 0, 2)).reshape(1, Bs * nq, CONV_CH)
    y_s = _ffn(xs, attn_s.reshape(1, Bs * nq, D_ATTN).astype(BF16), conv_s,
               (mod_s[2], mod_s[4], mod_s[3], mod_s[5]), wts, g_final)
    y_sample = y_s.reshape(Bs, nq, D)
    k_sample = k_s.reshape(1, Bs, nq, N_HEADS, HEAD_DIM)
    v_sample = v_s.reshape(1, Bs, nq, N_HEADS, HEAD_DIM)
    conv_sample = jnp.transpose(nstate_t, (1, 0, 2))[None]

    return (y_prompt, y_sample, k_prompt, v_prompt, conv_prompt, k_sample, v_sample, conv_sample)
```

```python
import functools

import jax
import jax.numpy as jnp
from jax import lax
from jax.experimental import pallas as pl
from jax.experimental.pallas import tpu as pltpu

F32 = jnp.float32
BF16 = jnp.bfloat16

D_MODEL = 1024
N_HEADS = 8
HEAD_DIM = 64
D_ATTN = N_HEADS * HEAD_DIM
CONV_CH = D_MODEL - D_ATTN
CONV_WIDTH = 31
CONV_HIST = CONV_WIDTH - 1
D_IN = 3 * D_ATTN + 2 * CONV_CH
N_EXPERTS = 32
TOP_K = 4
D_FF = D_MODEL
SWIGLU_LIMIT = 7.0
SWIGLU_ALPHA = 1.702
EPS = 1e-5
PAGE_SIZE = 128

LANES = 128
SUBLANES = 8
V7X_VMEM_BYTES = 64 * 1024 * 1024
VMEM_LIMIT = 48 * 1024 * 1024

ROW_TILE = 512
ATTN_TILE = 256
ATTN_TILES_PER_ITER = 2
CONV_TILE = 256
ROUTE_TILE = 512
MOE_TILE = 512
COMBINE_TILE = 256
PAGES_PER_STEP = 8
NEG_BIG = -1e30


def _params(*sem):
    return pltpu.CompilerParams(dimension_semantics=sem, vmem_limit_bytes=VMEM_LIMIT)


def _sigmoid(x):
    return 1.0 / (1.0 + jnp.exp(-x))


def _log_sigmoid_neg(z):
    return -(jnp.maximum(z, 0.0) + jnp.log(1.0 + jnp.exp(-jnp.abs(z))))


def _split_bf16(x):
    hi = x.astype(BF16)
    lo = (x - hi.astype(F32)).astype(BF16)
    return hi, lo


def _upper_incl(n):
    r = lax.broadcasted_iota(jnp.int32, (n, n), 0)
    c = lax.broadcasted_iota(jnp.int32, (n, n), 1)
    return jnp.where(r >= c, 1.0, 0.0).astype(BF16)


def _dot(a, b):
    return jnp.dot(a, b, preferred_element_type=F32)


def _dot_t(a, b):
    return lax.dot_general(a, b, (((1,), (1,)), ((), ())), preferred_element_type=F32)


def _ada_kernel(c_ref, w_ref, b_ref, o_ref):
    c = c_ref[...]
    s = (c * _sigmoid(c)).astype(BF16)
    o_ref[...] = _dot(s, w_ref[...].astype(BF16)) + b_ref[...]


def _ada(c_all, w_ada, b_ada):
    n, d = c_all.shape
    nout = w_ada.shape[1]
    tn = 1024
    return pl.pallas_call(
        _ada_kernel,
        grid=(nout // tn,),
        in_specs=[pl.BlockSpec((n, d), lambda j: (0, 0)),
                  pl.BlockSpec((d, tn), lambda j: (0, j)),
                  pl.BlockSpec((1, tn), lambda j: (0, j))],
        out_specs=pl.BlockSpec((n, tn), lambda j: (0, j)),
        out_shape=jax.ShapeDtypeStruct((n, nout), F32),
        compiler_params=_params("arbitrary"),
        name="ada",
    )(c_all, w_ada, b_ada.reshape(1, nout))


def _inproj_kernel(x_ref, sc_ref, sh_ref, g_ref, w_ref,
                   q_ref, k_ref, v_ref, kb_ref, vb_ref, glu_ref):
    x = x_ref[0]
    ms = jnp.mean(x * x, axis=-1, keepdims=True)
    h = x * lax.rsqrt(ms + EPS) * g_ref[...]
    h = (h * (1.0 + sc_ref[0]) + sh_ref[0]).astype(BF16)
    c = D_ATTN
    q_ref[0] = (_dot(h, w_ref[:, 0:c]) * (HEAD_DIM ** -0.5)).astype(BF16)
    k = _dot(h, w_ref[:, c:2 * c])
    k_ref[0] = k
    kb_ref[0] = k.astype(BF16)
    v = _dot(h, w_ref[:, 2 * c:3 * c])
    v_ref[0] = v
    vb_ref[0] = v.astype(BF16)
    a = _dot(h, w_ref[:, 3 * c:3 * c + CONV_CH])
    g = _dot(h, w_ref[:, 3 * c + CONV_CH:])
    glu_ref[0] = a * _sigmoid(g)


def _mod_spec(mod, ts):
    if mod.shape[1] == 1:
        return pl.BlockSpec((1, 1, D_MODEL), lambda g, t: (g, 0, 0))
    return pl.BlockSpec((1, ts, D_MODEL), lambda g, t: (g, t, 0))


def _inproj(x, sc, sh, g_norm, w_in_bf):
    G, R, D = x.shape
    ts = min(ROW_TILE, R)
    tok = lambda n: pl.BlockSpec((1, ts, n), lambda g, t: (g, t, 0))
    shp = lambda n, dt: jax.ShapeDtypeStruct((G, R, n), dt)
    return pl.pallas_call(
        _inproj_kernel,
        grid=(G, R // ts),
        in_specs=[tok(D), _mod_spec(sc, ts), _mod_spec(sh, ts),
                  pl.BlockSpec((1, D), lambda g, t: (0, 0)),
                  pl.BlockSpec((D, D_IN), lambda g, t: (0, 0))],
        out_specs=[tok(D_ATTN)] * 5 + [tok(CONV_CH)],
        out_shape=[shp(D_ATTN, BF16), shp(D_ATTN, F32), shp(D_ATTN, F32),
                   shp(D_ATTN, BF16), shp(D_ATTN, BF16), shp(CONV_CH, F32)],
        compiler_params=_params("arbitrary", "arbitrary"),
        name="inproj",
    )(x, sc, sh, g_norm.reshape(1, D), w_in_bf)


def _sb_scores(z, upper_incl, valid):
    m = _log_sigmoid_neg(z)
    if valid is not None:
        m = jnp.where(valid, m, 0.0)
    suffix = _dot(m.astype(BF16), upper_incl)
    return z + suffix, suffix[:, 0:1]


def _sb_weights(expo, carry, valid):
    a = jnp.exp(expo + carry)
    if valid is not None:
        a = jnp.where(valid, a, 0.0)
    return a.astype(BF16)


def _attn_prompt_kernel(bias_ref, q_ref, k_ref, v_ref, o_ref, acc_e, acc_o, car_e, car_o):
    t = ATTN_TILE
    p = pl.program_id(1)
    qi = pl.program_id(2)
    q2 = q_ref[0]
    lane = lax.broadcasted_iota(jnp.int32, (1, LANES), 1)
    even = lane < HEAD_DIM
    q_e = jnp.where(even, q2, jnp.zeros_like(q2))
    q_o = jnp.where(even, jnp.zeros_like(q2), q2)
    b_e = bias_ref[2 * p]
    b_o = bias_ref[2 * p + 1]
    upper = _upper_incl(t)

    def tiles(kbs, valids):
        ce, co = car_e[...], car_o[...]
        oe = oo = None
        for kb, valid in zip(kbs, valids):
            st = pl.multiple_of(kb * t, t)
            kblk = k_ref[0, pl.ds(st, t), :]
            vblk = v_ref[0, pl.ds(st, t), :]
            xe, se = _sb_scores(_dot_t(q_e, kblk) + b_e, upper, valid)
            xo, so = _sb_scores(_dot_t(q_o, kblk) + b_o, upper, valid)
            de = _dot(_sb_weights(xe, ce, valid), vblk)
            do = _dot(_sb_weights(xo, co, valid), vblk)
            oe = de if oe is None else oe + de
            oo = do if oo is None else oo + do
            ce = ce + se
            co = co + so
        acc_e[...] += oe
        acc_o[...] += oo
        car_e[...] = ce
        car_o[...] = co

    acc_e[...] = jnp.zeros_like(acc_e)
    acc_o[...] = jnp.zeros_like(acc_o)
    car_e[...] = jnp.zeros_like(car_e)
    car_o[...] = jnp.zeros_like(car_o)

    r = lax.broadcasted_iota(jnp.int32, (t, t), 0)
    c = lax.broadcasted_iota(jnp.int32, (t, t), 1)
    diag = c < r

    @pl.when(qi == 0)
    def _():
        tiles([0], [diag])

    @pl.when(qi > 0)
    def _():
        tiles([qi, qi - 1], [diag, None])

    n = ATTN_TILES_PER_ITER
    rest = jnp.maximum(qi - 1, 0)

    def body(i, _):
        kb = rest - 1 - n * i
        tiles([kb - j for j in range(n)], [None] * n)
        return 0

    lax.fori_loop(0, rest // n, body, 0)

    def tail(i, _):
        tiles([rest % n - 1 - i], [None])
        return 0

    lax.fori_loop(0, rest % n, tail, 0)

    o_ref[0] = jnp.where(even, acc_e[...], acc_o[...]).astype(o_ref.dtype)


def _attn_prompt(q, kb, vb, sb_bias):
    B, S, _ = q.shape
    t = ATTN_TILE
    npair = N_HEADS // 2
    return pl.pallas_call(
        _attn_prompt_kernel,
        grid_spec=pltpu.PrefetchScalarGridSpec(
            num_scalar_prefetch=0,
            grid=(B, npair, S // t),
            in_specs=[pl.BlockSpec(memory_space=pltpu.SMEM),
                      pl.BlockSpec((1, t, LANES), lambda b, p, i: (b, i, p)),
                      pl.BlockSpec((1, S, LANES), lambda b, p, i: (b, 0, p)),
                      pl.BlockSpec((1, S, LANES), lambda b, p, i: (b, 0, p))],
            out_specs=pl.BlockSpec((1, t, LANES), lambda b, p, i: (b, i, p)),
            scratch_shapes=[pltpu.VMEM((t, LANES), F32), pltpu.VMEM((t, LANES), F32),
                            pltpu.VMEM((t, 1), F32), pltpu.VMEM((t, 1), F32)]),
        out_shape=jax.ShapeDtypeStruct((B, S, D_ATTN), BF16),
        compiler_params=_params("arbitrary", "arbitrary", "arbitrary"),
        name="attn_prompt",
    )(sb_bias, q, kb, vb)


CONV_PAD = 32
CONV_TIME = 64


def _conv_prompt_kernel(glu_ref, w_ref, b_ref, g_ref, beta_ref, o_ref, ext_ref, y_ref, sh_ref):
    S = glu_ref.shape[1]
    tt = CONV_TIME
    ext_ref[0:CONV_PAD, :] = jnp.zeros((CONV_PAD, CONV_CH), F32)
    ext_ref[CONV_PAD:CONV_PAD + S, :] = glu_ref[0]
    off = CONV_PAD - CONV_HIST

    def conv_tile(i, _):
        t0 = pl.multiple_of(i * tt, tt)
        for cb in range(CONV_CH // LANES):
            cs = slice(cb * LANES, (cb + 1) * LANES)
            win = ext_ref[pl.ds(t0, tt + CONV_PAD), cs]
            acc = jnp.zeros((tt, LANES), F32) + b_ref[:, cs]
            for res in range(SUBLANES):
                taps = [w for w in range(CONV_WIDTH) if (off + w) % SUBLANES == res]
                span = max(off + w - res for w in taps) + tt
                sh_ref[res, 0:span, :] = win[res:res + span, :]
                for w in taps:
                    base = off + w - res
                    acc = acc + sh_ref[res, base:base + tt, :] * w_ref[w:w + 1, cs]
            y_ref[pl.ds(t0, tt), cs] = acc
        return 0

    lax.fori_loop(0, S // tt, conv_tile, 0)

    def norm_tile(i, _):
        t0 = pl.multiple_of(i * tt, tt)
        acc = y_ref[pl.ds(t0, tt), :]
        mu = jnp.mean(acc, axis=-1, keepdims=True)
        d = acc - mu
        var = jnp.mean(d * d, axis=-1, keepdims=True)
        y = d * lax.rsqrt(var + EPS) * g_ref[...] + beta_ref[...]
        o_ref[0, pl.ds(t0, tt), :] = (y * _sigmoid(y)).astype(o_ref.dtype)
        return 0

    lax.fori_loop(0, S // tt, norm_tile, 0)


def _conv_prompt(glu, w_dw, b_dw, ln_g, ln_b):
    B, S, C = glu.shape
    row = lambda: pl.BlockSpec((1, C), lambda b: (0, 0))
    return pl.pallas_call(
        _conv_prompt_kernel,
        grid=(B,),
        in_specs=[pl.BlockSpec((1, S, C), lambda b: (b, 0, 0)),
                  pl.BlockSpec((CONV_WIDTH, C), lambda b: (0, 0)),
                  row(), row(), row()],
        out_specs=pl.BlockSpec((1, S, C), lambda b: (b, 0, 0)),
        out_shape=jax.ShapeDtypeStruct((B, S, C), BF16),
        scratch_shapes=[pltpu.VMEM((S + CONV_PAD, C), F32), pltpu.VMEM((S, C), F32),
                        pltpu.VMEM((SUBLANES, CONV_TIME + CONV_PAD, LANES), F32)],
        compiler_params=_params("arbitrary"),
        name="conv_prompt",
    )(glu, w_dw, b_dw.reshape(1, C), ln_g.reshape(1, C), ln_b.reshape(1, C))


def _conv_sample_kernel(st_ref, glu_ref, w_ref, b_ref, g_ref, beta_ref, o_ref, nst_ref):
    nq = glu_ref.shape[0]

    def ext(j):
        return st_ref[j] if j < CONV_HIST else glu_ref[j - CONV_HIST]

    for i in range(nq):
        acc = jnp.zeros(st_ref.shape[1:], F32) + b_ref[...]
        for w in range(CONV_WIDTH):
            acc = acc + ext(i + w) * w_ref[w:w + 1, :]
        mu = jnp.mean(acc, axis=-1, keepdims=True)
        d = acc - mu
        var = jnp.mean(d * d, axis=-1, keepdims=True)
        y = d * lax.rsqrt(var + EPS) * g_ref[...] + beta_ref[...]
        o_ref[i] = (y * _sigmoid(y)).astype(o_ref.dtype)
    for j in range(CONV_HIST):
        nst_ref[j] = ext(j + nq)


def _conv_sample(state_t, glu_t, w_dw, b_dw, ln_g, ln_b):
    H, Bs, C = state_t.shape
    nq = glu_t.shape[0]
    full = lambda shape: pl.BlockSpec(shape, lambda i: (0,) * len(shape))
    return pl.pallas_call(
        _conv_sample_kernel,
        grid=(1,),
        in_specs=[full((H, Bs, C)), full((nq, Bs, C)), full((CONV_WIDTH, C)),
                  full((1, C)), full((1, C)), full((1, C))],
        out_specs=[full((nq, Bs, C)), full((H, Bs, C))],
        out_shape=[jax.ShapeDtypeStruct((nq, Bs, C), BF16),
                   jax.ShapeDtypeStruct((H, Bs, C), F32)],
        compiler_params=_params("arbitrary"),
        name="conv_sample",
    )(state_t, glu_t, w_dw, b_dw.reshape(1, C), ln_g.reshape(1, C), ln_b.reshape(1, C))


def _attn_sample_kernel(pt_ref, q_ref, kn_ref, vn_ref, bias_ref, *rest):
    G = PAGES_PER_STEP
    k_refs = rest[:G]
    v_refs = rest[G:2 * G]
    o_ref = rest[2 * G]
    acc_ref, car_ref = rest[2 * G + 1:]
    s = pl.program_id(1)
    nq = q_ref.shape[1]
    rows = nq * N_HEADS

    hrow = lax.broadcasted_iota(jnp.int32, (N_HEADS, D_ATTN), 0)
    hcol = lax.broadcasted_iota(jnp.int32, (N_HEADS, D_ATTN), 1) // HEAD_DIM
    headmask = hrow == hcol
    q = q_ref[0]
    qbd = jnp.concatenate(
        [jnp.where(headmask, q[i:i + 1, :], 0.0) for i in range(nq)], axis=0).astype(BF16)
    bias = bias_ref[...]
    upper = _upper_incl(PAGE_SIZE)

    @pl.when(s == 0)
    def _():
        r = lax.broadcasted_iota(jnp.int32, (rows, PAGE_SIZE), 0) // N_HEADS
        c = lax.broadcasted_iota(jnp.int32, (rows, PAGE_SIZE), 1)
        valid = c < r
        expo, rowsum = _sb_scores(_dot(qbd, kn_ref[0]) + bias, upper, valid)
        acc_ref[...] = _dot_t(_sb_weights(expo, 0.0, valid), vn_ref[0])
        car_ref[...] = rowsum

    car = car_ref[...]
    acc = acc_ref[...]
    kts = [k_refs[j][0].reshape(D_ATTN, PAGE_SIZE).astype(BF16) for j in range(G)]
    zs = [_dot(qbd, kt) + bias for kt in kts]
    scores = [_sb_scores(z, upper, None) for z in zs]
    for j, (expo, rowsum) in enumerate(scores):
        vt = v_refs[j][0].reshape(D_ATTN, PAGE_SIZE).astype(BF16)
        acc = acc + _dot_t(_sb_weights(expo, car, None), vt)
        car = car + rowsum
    acc_ref[...] = acc
    car_ref[...] = car

    @pl.when(s == pl.num_programs(1) - 1)
    def _():
        acc = acc_ref[...]
        out = [jnp.sum(jnp.where(headmask, acc[i * N_HEADS:(i + 1) * N_HEADS, :], 0.0),
                       axis=0, keepdims=True) for i in range(nq)]
        o_ref[0] = jnp.concatenate(out, axis=0).astype(o_ref.dtype)


def _attn_sample(q, knew_t, vnew_t, bias_rows, cache_kt, cache_vt, page_table):
    Bs, nq, _ = q.shape
    n_pages = page_table.shape[1]
    G = PAGES_PER_STEP
    assert n_pages % G == 0
    rows = nq * N_HEADS

    def page_spec(j):
        return pl.BlockSpec(
            (1, N_HEADS, HEAD_DIM, PAGE_SIZE),
            lambda b, s, pt, j=j: (pt[b, n_pages - 1 - (s * G + j)], 0, 0, 0))

    per_b = lambda shape: pl.BlockSpec((1,) + shape, lambda b, s, pt: (b, 0, 0))
    return pl.pallas_call(
        _attn_sample_kernel,
        grid_spec=pltpu.PrefetchScalarGridSpec(
            num_scalar_prefetch=1,
            grid=(Bs, n_pages // G),
            in_specs=[per_b((nq, D_ATTN)), per_b((D_ATTN, PAGE_SIZE)), per_b((D_ATTN, PAGE_SIZE)),
                      pl.BlockSpec((rows, 1), lambda b, s, pt: (0, 0))]
                     + [page_spec(j) for j in range(G)] * 2,
            out_specs=per_b((nq, D_ATTN)),
            scratch_shapes=[pltpu.VMEM((rows, D_ATTN), F32), pltpu.VMEM((rows, 1), F32)]),
        out_shape=jax.ShapeDtypeStruct((Bs, nq, D_ATTN), F32),
        compiler_params=_params("arbitrary", "arbitrary"),
        name="attn_sample",
    )(page_table, q, knew_t, vnew_t, bias_rows, *([cache_kt] * G), *([cache_vt] * G))


def _outproj_kernel(x_ref, a_ref, c_ref, ga_ref, sc_ref, sh_ref, g_ref, w_ref,
                    wr_hi_ref, wr_lo_ref, br_ref, x1_ref, h2_ref, lg_ref):
    mix = _dot(a_ref[0], w_ref[0:D_ATTN, :]) + _dot(c_ref[0], w_ref[D_ATTN:, :])
    x1 = x_ref[0] + ga_ref[0] * mix
    x1_ref[0] = x1
    ms = jnp.mean(x1 * x1, axis=-1, keepdims=True)
    h2 = x1 * lax.rsqrt(ms + EPS) * g_ref[...]
    h2 = h2 * (1.0 + sc_ref[0]) + sh_ref[0]
    h2_ref[0] = h2
    hi, lo = _split_bf16(h2)
    lg_ref[0] = (_dot(hi, wr_hi_ref[...]) + _dot(lo, wr_hi_ref[...])
                 + _dot(hi, wr_lo_ref[...]) + br_ref[...])


def _outproj(x, attn, conv, ga, sc, sh, g_norm, w_out_bf, wr_hi, wr_lo, br_pad):
    G, R, D = x.shape
    ts = min(ROW_TILE, R)
    tok = lambda n: pl.BlockSpec((1, ts, n), lambda g, t: (g, t, 0))
    const = lambda shape: pl.BlockSpec(shape, lambda g, t: (0, 0))
    return pl.pallas_call(
        _outproj_kernel,
        grid=(G, R // ts),
        in_specs=[tok(D), tok(D_ATTN), tok(CONV_CH), _mod_spec(ga, ts), _mod_spec(sc, ts),
                  _mod_spec(sh, ts), const((1, D)), const((D, D)),
                  const((D, LANES)), const((D, LANES)), const((1, LANES))],
        out_specs=[tok(D), tok(D), tok(LANES)],
        out_shape=[jax.ShapeDtypeStruct((G, R, D), F32), jax.ShapeDtypeStruct((G, R, D), F32),
                   jax.ShapeDtypeStruct((G, R, LANES), F32)],
        compiler_params=_params("arbitrary", "arbitrary"),
        name="outproj",
    )(x, attn, conv, ga, sc, sh, g_norm.reshape(1, D), w_out_bf, wr_hi, wr_lo, br_pad)


def _route_kernel(lg_ref, dest_ref, gate_ref, cnt_ref, counts, offs, run):
    ph = pl.program_id(0)
    t = pl.program_id(1)
    tr = lg_ref.shape[0]
    lane = lax.broadcasted_iota(jnp.int32, (tr, LANES), 1).astype(F32)
    l = lg_ref[...]
    onehots, vals = [], []
    for _ in range(TOP_K):
        mx = jnp.max(l, axis=-1, keepdims=True)
        idx = jnp.min(jnp.where(l == mx, lane, float(LANES)), axis=-1, keepdims=True)
        oh = lane == idx
        onehots.append(oh)
        vals.append(mx)
        l = jnp.where(oh, NEG_BIG, l)
    total = jnp.zeros((tr, LANES), F32)
    for oh in onehots:
        total = total + jnp.where(oh, 1.0, 0.0)
    colsum = jnp.sum(total, axis=0, keepdims=True)

    @pl.when(jnp.logical_and(ph == 0, t == 0))
    def _():
        counts[...] = jnp.zeros_like(counts)

    @pl.when(ph == 0)
    def _():
        counts[...] += colsum

    @pl.when(jnp.logical_and(ph == 1, t == 0))
    def _():
        c = jnp.broadcast_to(counts[...], (8, LANES))
        c1 = c.astype(BF16)
        r1 = c - c1.astype(F32)
        c2 = r1.astype(BF16)
        c3 = (r1 - c2.astype(F32)).astype(BF16)
        rr = lax.broadcasted_iota(jnp.int32, (LANES, LANES), 0)
        cc = lax.broadcasted_iota(jnp.int32, (LANES, LANES), 1)
        before = jnp.where(rr < cc, 1.0, 0.0).astype(BF16)
        o = _dot(c1, before) + _dot(c2, before) + _dot(c3, before)
        offs[...] = o[0:1, :]
        run[...] = jnp.zeros_like(run)
        cnt_ref[...] = counts[...]

    @pl.when(ph == 1)
    def _():
        rr = lax.broadcasted_iota(jnp.int32, (tr, tr), 0)
        cc = lax.broadcasted_iota(jnp.int32, (tr, tr), 1)
        earlier = jnp.where(cc < rr, 1.0, 0.0).astype(BF16)
        pos = _dot(earlier, total.astype(BF16)) + (offs[...] + run[...])
        kcol = lax.broadcasted_iota(jnp.int32, (tr, LANES), 1)
        dest = jnp.zeros((tr, LANES), F32)
        gate = jnp.zeros((tr, LANES), F32)
        es = [jnp.exp(v - vals[0]) for v in vals]
        den = es[0] + es[1] + es[2] + es[3]
        for k in range(TOP_K):
            dk = jnp.sum(jnp.where(onehots[k], pos, 0.0), axis=-1, keepdims=True)
            dest = jnp.where(kcol == k, dk, dest)
            gate = jnp.where(kcol == k, es[k] / den, gate)
        dest_ref[...] = dest[:, 0:TOP_K].astype(jnp.int32)
        gate_ref[...] = gate[:, 0:TOP_K]
        run[...] += colsum


def _route(logits):
    T = logits.shape[0]
    tr = min(ROUTE_TILE, T)
    return pl.pallas_call(
        _route_kernel,
        grid=(2, T // tr),
        in_specs=[pl.BlockSpec((tr, LANES), lambda ph, t: (t, 0))],
        out_specs=[pl.BlockSpec((tr, TOP_K), lambda ph, t: (ph * t, 0)),
                   pl.BlockSpec((tr, TOP_K), lambda ph, t: (ph * t, 0)),
                   pl.BlockSpec((1, LANES), lambda ph, t: (0, 0))],
        out_shape=[jax.ShapeDtypeStruct((T, TOP_K), jnp.int32),
                   jax.ShapeDtypeStruct((T, TOP_K), F32),
                   jax.ShapeDtypeStruct((1, LANES), F32)],
        scratch_shapes=[pltpu.VMEM((1, LANES), F32)] * 3,
        compiler_params=_params("arbitrary", "arbitrary"),
        name="route",
    )(logits)


def _row_copy(src, dst, sem):
    return pltpu.make_async_copy(src, dst, sem)


def _dispatch_kernel(dest_ref, h_ref, xs_ref, sem):
    td = h_ref.shape[0]

    def issue(t, _):
        for k in range(TOP_K):
            d = dest_ref[0, 0, t * TOP_K + k]
            _row_copy(h_ref.at[pl.ds(t, 1), :], xs_ref.at[pl.ds(d, 1), :], sem).start(priority=k % 2)
        return 0

    lax.fori_loop(0, td, issue, 0)

    def drain(t, _):
        for k in range(TOP_K):
            _row_copy(h_ref.at[pl.ds(0, 1), :], xs_ref.at[pl.ds(0, 1), :], sem).wait()
        return 0

    lax.fori_loop(0, td, drain, 0)


def _dispatch(h2, dest):
    T, D = h2.shape
    td = min(ROUTE_TILE, T)
    dest3 = dest.reshape(T // td, 1, td * TOP_K)
    return pl.pallas_call(
        _dispatch_kernel,
        grid=(T // td,),
        in_specs=[pl.BlockSpec((1, 1, td * TOP_K), lambda i: (i, 0, 0), memory_space=pltpu.SMEM),
                  pl.BlockSpec((td, D), lambda i: (i, 0))],
        out_specs=pl.BlockSpec(memory_space=pl.ANY),
        out_shape=jax.ShapeDtypeStruct((T * TOP_K, D), h2.dtype),
        scratch_shapes=[pltpu.SemaphoreType.DMA(())],
        compiler_params=_params("arbitrary"),
        name="dispatch",
    )(dest3, h2)


MOE_FF_CHUNK = 256


def _moe_kernel(blk_ref, exp_ref, lo_ref, hi_ref, first_ref, n_ref,
                x_ref, wgu_ref, bgu_ref, wd_ref, bd_ref, o_ref):
    w = pl.program_id(0)
    bm = x_ref.shape[0]

    @pl.when(w < n_ref[0])
    def _():
        x = x_ref[...].astype(BF16)
        res = jnp.zeros((bm, D_MODEL), F32) + bd_ref[0]
        for n in range(D_FF // MOE_FF_CHUNK):
            c0 = n * MOE_FF_CHUNK
            c1 = c0 + MOE_FF_CHUNK
            g = _dot(x, wgu_ref[0, :, c0:c1]) + bgu_ref[0, :, c0:c1]
            u = _dot(x, wgu_ref[0, :, D_FF + c0:D_FF + c1]) + bgu_ref[0, :, D_FF + c0:D_FF + c1]
            g = jnp.minimum(g, SWIGLU_LIMIT)
            u = jnp.clip(u, -SWIGLU_LIMIT, SWIGLU_LIMIT)
            act = (u + 1.0) * (g * _sigmoid(SWIGLU_ALPHA * g))
            res = res + _dot(act.astype(BF16), wd_ref[0, c0:c1, :])

        @pl.when(first_ref[w] == 1)
        def _():
            o_ref[...] = res

        @pl.when(first_ref[w] == 0)
        def _():
            row = blk_ref[w] * bm + lax.broadcasted_iota(jnp.int32, (bm, 1), 0)
            mine = jnp.logical_and(row >= lo_ref[w], row < hi_ref[w])
            o_ref[...] = jnp.where(mine, res, o_ref[...])


def _moe_schedule(counts, n_rows, bm):
    c = counts.astype(jnp.int32)
    end = jnp.cumsum(c)
    start = end - c
    first_blk = start // bm
    last_blk = jnp.where(c > 0, (end - 1) // bm, first_blk - 1)
    n_items_e = last_blk - first_blk + 1
    item_end = jnp.cumsum(n_items_e)
    item_start = item_end - n_items_e
    n_items = item_end[-1]
    w_max = n_rows // bm + N_EXPERTS - 1
    w = jnp.minimum(jnp.arange(w_max, dtype=jnp.int32), n_items - 1)
    e = jnp.sum((item_end[None, :] <= w[:, None]).astype(jnp.int32), axis=1)
    blk = first_blk[e] + (w - item_start[e])
    prev = jnp.concatenate([jnp.full((1,), -1, jnp.int32), blk[:-1]])
    first = (blk != prev).astype(jnp.int32)
    return blk, e, start[e], end[e], first, n_items.reshape(1)


def _moe(xs, counts, wgu_bf, bgu, wd_bf, bd):
    R, D = xs.shape
    bm = min(MOE_TILE, R)
    blk, e, lo, hi, first, n_items = _moe_schedule(counts, R, bm)
    w_max = blk.shape[0]
    return pl.pallas_call(
        _moe_kernel,
        grid_spec=pltpu.PrefetchScalarGridSpec(
            num_scalar_prefetch=6,
            grid=(w_max,),
            in_specs=[pl.BlockSpec((bm, D), lambda w, blk, e, *_: (blk[w], 0)),
                      pl.BlockSpec((1, D, 2 * D_FF), lambda w, blk, e, *_: (e[w], 0, 0)),
                      pl.BlockSpec((1, 1, 2 * D_FF), lambda w, blk, e, *_: (e[w], 0, 0)),
                      pl.BlockSpec((1, D_FF, D), lambda w, blk, e, *_: (e[w], 0, 0)),
                      pl.BlockSpec((1, 1, D), lambda w, blk, e, *_: (e[w], 0, 0))],
            out_specs=pl.BlockSpec((bm, D), lambda w, blk, e, *_: (blk[w], 0))),
        out_shape=jax.ShapeDtypeStruct((R, D), F32),
        compiler_params=_params("arbitrary"),
        name="moe",
    )(blk, e, lo, hi, first, n_items, xs, wgu_bf,
      bgu.reshape(N_EXPERTS, 1, 2 * D_FF), wd_bf, bd.reshape(N_EXPERTS, 1, D))


def _combine_kernel(dest_ref, gate_ref, x1_ref, ga_ref, g_ref, ys_ref, o_ref, buf, sem):
    tc = x1_ref.shape[1]

    def issue(t, _):
        for k in range(TOP_K):
            d = dest_ref[0, 0, 0, t * TOP_K + k]
            _row_copy(ys_ref.at[pl.ds(d, 1), :], buf.at[k, pl.ds(t, 1), :], sem).start(priority=k % 2)
        return 0

    lax.fori_loop(0, tc, issue, 0)

    def drain(t, _):
        for k in range(TOP_K):
            _row_copy(ys_ref.at[pl.ds(0, 1), :], buf.at[0, pl.ds(0, 1), :], sem).wait()
        return 0

    lax.fori_loop(0, tc, drain, 0)
    gate = gate_ref[0]
    ff = jnp.zeros((tc, D_MODEL), F32)
    for k in range(TOP_K):
        ff = ff + buf[k] * gate[:, k:k + 1]
    x2 = x1_ref[0] + ga_ref[0] * ff
    ms = jnp.mean(x2 * x2, axis=-1, keepdims=True)
    o_ref[0] = x2 * lax.rsqrt(ms + EPS) * g_ref[...]


def _combine(ys, dest, gates, x1, ga, g_final):
    G, R, D = x1.shape
    tc = min(COMBINE_TILE, R)
    nt = R // tc
    dest4 = dest.reshape(G, nt, 1, tc * TOP_K)
    gates3 = gates.reshape(G, R, TOP_K)
    return pl.pallas_call(
        _combine_kernel,
        grid=(G, nt),
        in_specs=[pl.BlockSpec((1, 1, 1, tc * TOP_K), lambda g, t: (g, t, 0, 0),
                               memory_space=pltpu.SMEM),
                  pl.BlockSpec((1, tc, TOP_K), lambda g, t: (g, t, 0)),
                  pl.BlockSpec((1, tc, D), lambda g, t: (g, t, 0)),
                  _mod_spec(ga, tc),
                  pl.BlockSpec((1, D), lambda g, t: (0, 0)),
                  pl.BlockSpec(memory_space=pl.ANY)],
        out_specs=pl.BlockSpec((1, tc, D), lambda g, t: (g, t, 0)),
        out_shape=jax.ShapeDtypeStruct((G, R, D), F32),
        scratch_shapes=[pltpu.VMEM((TOP_K, tc, D), F32), pltpu.SemaphoreType.DMA(())],
        compiler_params=_params("arbitrary", "arbitrary"),
        name="combine",
    )(dest4, gates3, x1, ga, g_final.reshape(1, D), ys)


def _ffn(x, attn, conv, mods, wts, g_final):
    ga1, sc2, sh2, ga2 = mods
    G, R, D = x.shape
    x1, h2, logits = _outproj(x, attn, conv, ga1, sc2, sh2, wts["g_ffn"], wts["w_out"],
                              wts["wr_hi"], wts["wr_lo"], wts["br"])
    T = G * R
    dest, gates, counts = _route(logits.reshape(T, LANES))
    xs = _dispatch(h2.reshape(T, D), dest)
    ys = _moe(xs, counts[0, :N_EXPERTS], wts["wgu"], wts["bgu"], wts["wd"], wts["bd"])
    return _combine(ys, dest, gates, x1, ga2, g_final)


def kernel(x_prompt, x_sample, c_prompt, c_sample, cache_k, cache_v, state_conv, page_table,
           g_attn_norm, g_ffn_norm, w_ada, b_ada, w_in, sb_bias, w_dw, b_dw, ln_conv_g, ln_conv_b,
           w_out, w_router, b_router, w_gate_up, b_gate_up, w_down, b_down, g_final):
    assert w_ada.shape[0] == 1, "one trunk layer"
    B, S, D = x_prompt.shape
    Bs, nq, _ = x_sample.shape
    n_pool = cache_k.shape[1]

    wr = jnp.pad(w_router[0], ((0, 0), (0, LANES - N_EXPERTS)))
    wr_hi = wr.astype(BF16)
    wts = dict(
        g_ffn=g_ffn_norm[0], w_out=w_out[0].astype(BF16),
        wr_hi=wr_hi, wr_lo=(wr - wr_hi.astype(F32)).astype(BF16),
        br=jnp.pad(b_router[0], (0, LANES - N_EXPERTS), constant_values=NEG_BIG).reshape(1, LANES),
        wgu=w_gate_up[0].astype(BF16), bgu=b_gate_up[0],
        wd=w_down[0].astype(BF16), bd=b_down[0])
    w_in_bf = w_in[0].astype(BF16)

    ada = _ada(jnp.concatenate([c_prompt, c_sample], axis=0), w_ada[0], b_ada[0])
    ada_p = ada[:B].reshape(B, 1, 6, D)
    ada_s = jnp.broadcast_to(ada[B:].reshape(Bs, 1, 6, D), (Bs, nq, 6, D)).reshape(1, Bs * nq, 6, D)
    mod_p = [ada_p[:, :, i, :] for i in range(6)]
    mod_s = [ada_s[:, :, i, :] for i in range(6)]

    q, k, v, kb, vb, glu = _inproj(x_prompt, mod_p[1], mod_p[0], g_attn_norm[0], w_in_bf)
    attn = _attn_prompt(q, kb, vb, sb_bias[0])
    conv = _conv_prompt(glu, w_dw[0], b_dw[0], ln_conv_g[0], ln_conv_b[0])
    y_prompt = _ffn(x_prompt, attn, conv, (mod_p[2], mod_p[4], mod_p[3], mod_p[5]), wts, g_final)
    k_prompt = k.reshape(1, B, S, N_HEADS, HEAD_DIM)
    v_prompt = v.reshape(1, B, S, N_HEADS, HEAD_DIM)
    conv_prompt = glu[:, S - CONV_HIST:, :][None]

    xs = x_sample.reshape(1, Bs * nq, D)
    q_s, k_s, v_s, kb_s, vb_s, glu_s = _inproj(xs, mod_s[1], mod_s[0], g_attn_norm[0], w_in_bf)

    def new_page(a):
        a = jnp.transpose(a.reshape(Bs, nq, D_ATTN), (0, 2, 1))
        return jnp.pad(a, ((0, 0), (0, 0), (0, PAGE_SIZE - nq)))

    to_pages = lambda c: jnp.transpose(c[0], (0, 2, 3, 1))
    bias_rows = jnp.tile(sb_bias[0], nq).reshape(nq * N_HEADS, 1)
    attn_s = _attn_sample(q_s.reshape(Bs, nq, D_ATTN).astype(F32), new_page(kb_s), new_page(vb_s), bias_rows,
                          to_pages(cache_k), to_pages(cache_v), page_table)
    state_t = jnp.transpose(state_conv[0], (1, 0, 2))
    glu_t = jnp.transpose(glu_s.reshape(Bs, nq, CONV_CH), (1, 0, 2))
    conv_t, nstate_t = _conv_sample(state_t, glu_t, w_dw[0], b_dw[0], ln_conv_g[0], ln_conv_b[0])
    conv_s = jnp.transpose(conv_t, (1, 0, 2)).reshape(1, Bs * nq, CONV_CH)
    y_s = _ffn(xs, attn_s.reshape(1, Bs * nq, D_ATTN).astype(BF16), conv_s,
               (mod_s[2], mod_s[4], mod_s[3], mod_s[5]), wts, g_final)
    y_sample = y_s.reshape(Bs, nq, D)
    k_sample = k_s.reshape(1, Bs, nq, N_HEADS, HEAD_DIM)
    v_sample = v_s.reshape(1, Bs, nq, N_HEADS, HEAD_DIM)
    conv_sample = jnp.transpose(nstate_t, (1, 0, 2))[None]

    return (y_prompt, y_sample, k_prompt, v_prompt, conv_prompt, k_sample, v_sample, conv_sample)
```

```python
import functools

import jax
import jax.numpy as jnp
from jax import lax
from jax.experimental import pallas as pl
from jax.experimental.pallas import tpu as pltpu

F32 = jnp.float32
BF16 = jnp.bfloat16

D_MODEL = 1024
N_HEADS = 8
HEAD_DIM = 64
D_ATTN = N_HEADS * HEAD_DIM
CONV_CH = D_MODEL - D_ATTN
CONV_WIDTH = 31
CONV_HIST = CONV_WIDTH - 1
D_IN = 3 * D_ATTN + 2 * CONV_CH
N_EXPERTS = 32
TOP_K = 4
D_FF = D_MODEL
SWIGLU_LIMIT = 7.0
SWIGLU_ALPHA = 1.702
EPS = 1e-5
PAGE_SIZE = 128

LANES = 128
SUBLANES = 8
V7X_VMEM_BYTES = 64 * 1024 * 1024
VMEM_LIMIT = 48 * 1024 * 1024

ROW_TILE = 512
ATTN_TILE = 256
ATTN_TILES_PER_ITER = 2
CONV_TILE = 256
ROUTE_TILE = 512
MOE_TILE = 512
COMBINE_TILE = 256
PAGES_PER_STEP = 8
NEG_BIG = -1e30


def _params(*sem):
    return pltpu.CompilerParams(dimension_semantics=sem, vmem_limit_bytes=VMEM_LIMIT)


def _sigmoid(x):
    return 1.0 / (1.0 + jnp.exp(-x))


def _log_sigmoid_neg(z):
    return -(jnp.maximum(z, 0.0) + jnp.log(1.0 + jnp.exp(-jnp.abs(z))))


def _split_bf16(x):
    hi = x.astype(BF16)
    lo = (x - hi.astype(F32)).astype(BF16)
    return hi, lo


def _upper_incl(n):
    r = lax.broadcasted_iota(jnp.int32, (n, n), 0)
    c = lax.broadcasted_iota(jnp.int32, (n, n), 1)
    return jnp.where(r >= c, 1.0, 0.0).astype(BF16)


def _rows_to_tiles(ref_view, x):
    rows = x.shape[0]
    for s in range(SUBLANES):
        ref_view[pl.ds(s, rows, stride=SUBLANES), :] = x[:, s * LANES:(s + 1) * LANES]


def _tiles_to_rows(ref_view):
    rows = ref_view.shape[0] // SUBLANES
    return jnp.concatenate(
        [ref_view[pl.ds(s, rows, stride=SUBLANES), :] for s in range(SUBLANES)], axis=1)


def _tile_of(ref, row):
    return ref.at[pl.ds(pl.multiple_of(row * SUBLANES, SUBLANES), SUBLANES), :]


def _dot(a, b):
    return jnp.dot(a, b, preferred_element_type=F32)


def _dot_t(a, b):
    return lax.dot_general(a, b, (((1,), (1,)), ((), ())), preferred_element_type=F32)


def _ada_kernel(c_ref, w_ref, b_ref, o_ref):
    c = c_ref[...]
    s = (c * _sigmoid(c)).astype(BF16)
    o_ref[...] = _dot(s, w_ref[...].astype(BF16)) + b_ref[...]


def _ada(c_all, w_ada, b_ada):
    n, d = c_all.shape
    nout = w_ada.shape[1]
    tn = 1024
    return pl.pallas_call(
        _ada_kernel,
        grid=(nout // tn,),
        in_specs=[pl.BlockSpec((n, d), lambda j: (0, 0)),
                  pl.BlockSpec((d, tn), lambda j: (0, j)),
                  pl.BlockSpec((1, tn), lambda j: (0, j))],
        out_specs=pl.BlockSpec((n, tn), lambda j: (0, j)),
        out_shape=jax.ShapeDtypeStruct((n, nout), F32),
        compiler_params=_params("arbitrary"),
        name="ada",
    )(c_all, w_ada, b_ada.reshape(1, nout))


def _inproj_kernel(x_ref, sc_ref, sh_ref, g_ref, w_ref,
                   q_ref, k_ref, v_ref, kb_ref, vb_ref, glu_ref):
    x = x_ref[0]
    ms = jnp.mean(x * x, axis=-1, keepdims=True)
    h = x * lax.rsqrt(ms + EPS) * g_ref[...]
    h = (h * (1.0 + sc_ref[0]) + sh_ref[0]).astype(BF16)
    c = D_ATTN
    q_ref[0] = (_dot(h, w_ref[:, 0:c]) * (HEAD_DIM ** -0.5)).astype(BF16)
    k = _dot(h, w_ref[:, c:2 * c])
    k_ref[0] = k
    kb_ref[0] = k.astype(BF16)
    v = _dot(h, w_ref[:, 2 * c:3 * c])
    v_ref[0] = v
    vb_ref[0] = v.astype(BF16)
    a = _dot(h, w_ref[:, 3 * c:3 * c + CONV_CH])
    g = _dot(h, w_ref[:, 3 * c + CONV_CH:])
    glu_ref[0] = a * _sigmoid(g)


def _mod_spec(mod, ts):
    if mod.shape[1] == 1:
        return pl.BlockSpec((1, 1, D_MODEL), lambda g, t: (g, 0, 0))
    return pl.BlockSpec((1, ts, D_MODEL), lambda g, t: (g, t, 0))


def _inproj(x, sc, sh, g_norm, w_in_bf):
    G, R, D = x.shape
    ts = min(ROW_TILE, R)
    tok = lambda n: pl.BlockSpec((1, ts, n), lambda g, t: (g, t, 0))
    shp = lambda n, dt: jax.ShapeDtypeStruct((G, R, n), dt)
    return pl.pallas_call(
        _inproj_kernel,
        grid=(G, R // ts),
        in_specs=[tok(D), _mod_spec(sc, ts), _mod_spec(sh, ts),
                  pl.BlockSpec((1, D), lambda g, t: (0, 0)),
                  pl.BlockSpec((D, D_IN), lambda g, t: (0, 0))],
        out_specs=[tok(D_ATTN)] * 5 + [tok(CONV_CH)],
        out_shape=[shp(D_ATTN, BF16), shp(D_ATTN, F32), shp(D_ATTN, F32),
                   shp(D_ATTN, BF16), shp(D_ATTN, BF16), shp(CONV_CH, F32)],
        compiler_params=_params("arbitrary", "arbitrary"),
        name="inproj",
    )(x, sc, sh, g_norm.reshape(1, D), w_in_bf)


def _sb_scores(z, upper_incl, valid):
    m = _log_sigmoid_neg(z)
    if valid is not None:
        m = jnp.where(valid, m, 0.0)
    suffix = _dot(m.astype(BF16), upper_incl)
    return z + suffix, suffix[:, 0:1]


def _sb_weights(expo, carry, valid):
    a = jnp.exp(expo + carry)
    if valid is not None:
        a = jnp.where(valid, a, 0.0)
    return a.astype(BF16)


def _attn_prompt_kernel(bias_ref, q_ref, k_ref, v_ref, o_ref, acc_e, acc_o, car_e, car_o):
    t = ATTN_TILE
    p = pl.program_id(1)
    qi = pl.program_id(2)
    q2 = q_ref[0]
    lane = lax.broadcasted_iota(jnp.int32, (1, LANES), 1)
    even = lane < HEAD_DIM
    q_e = jnp.where(even, q2, jnp.zeros_like(q2))
    q_o = jnp.where(even, jnp.zeros_like(q2), q2)
    b_e = bias_ref[2 * p]
    b_o = bias_ref[2 * p + 1]
    upper = _upper_incl(t)

    def tiles(kbs, valids):
        ce, co = car_e[...], car_o[...]
        oe = oo = None
        for kb, valid in zip(kbs, valids):
            st = pl.multiple_of(kb * t, t)
            kblk = k_ref[0, pl.ds(st, t), :]
            vblk = v_ref[0, pl.ds(st, t), :]
            xe, se = _sb_scores(_dot_t(q_e, kblk) + b_e, upper, valid)
            xo, so = _sb_scores(_dot_t(q_o, kblk) + b_o, upper, valid)
            de = _dot(_sb_weights(xe, ce, valid), vblk)
            do = _dot(_sb_weights(xo, co, valid), vblk)
            oe = de if oe is None else oe + de
            oo = do if oo is None else oo + do
            ce = ce + se
            co = co + so
        acc_e[...] += oe
        acc_o[...] += oo
        car_e[...] = ce
        car_o[...] = co

    acc_e[...] = jnp.zeros_like(acc_e)
    acc_o[...] = jnp.zeros_like(acc_o)
    car_e[...] = jnp.zeros_like(car_e)
    car_o[...] = jnp.zeros_like(car_o)

    r = lax.broadcasted_iota(jnp.int32, (t, t), 0)
    c = lax.broadcasted_iota(jnp.int32, (t, t), 1)
    diag = c < r

    @pl.when(qi == 0)
    def _():
        tiles([0], [diag])

    @pl.when(qi > 0)
    def _():
        tiles([qi, qi - 1], [diag, None])

    n = ATTN_TILES_PER_ITER
    rest = jnp.maximum(qi - 1, 0)

    def body(i, _):
        kb = rest - 1 - n * i
        tiles([kb - j for j in range(n)], [None] * n)
        return 0

    lax.fori_loop(0, rest // n, body, 0)

    def tail(i, _):
        tiles([rest % n - 1 - i], [None])
        return 0

    lax.fori_loop(0, rest % n, tail, 0)

    o_ref[0] = jnp.where(even, acc_e[...], acc_o[...]).astype(o_ref.dtype)


def _attn_prompt(q, kb, vb, sb_bias):
    B, S, _ = q.shape
    t = ATTN_TILE
    npair = N_HEADS // 2
    return pl.pallas_call(
        _attn_prompt_kernel,
        grid_spec=pltpu.PrefetchScalarGridSpec(
            num_scalar_prefetch=0,
            grid=(B, npair, S // t),
            in_specs=[pl.BlockSpec(memory_space=pltpu.SMEM),
                      pl.BlockSpec((1, t, LANES), lambda b, p, i: (b, i, p)),
                      pl.BlockSpec((1, S, LANES), lambda b, p, i: (b, 0, p)),
                      pl.BlockSpec((1, S, LANES), lambda b, p, i: (b, 0, p))],
            out_specs=pl.BlockSpec((1, t, LANES), lambda b, p, i: (b, i, p)),
            scratch_shapes=[pltpu.VMEM((t, LANES), F32), pltpu.VMEM((t, LANES), F32),
                            pltpu.VMEM((t, 1), F32), pltpu.VMEM((t, 1), F32)]),
        out_shape=jax.ShapeDtypeStruct((B, S, D_ATTN), BF16),
        compiler_params=_params("arbitrary", "arbitrary", "arbitrary"),
        name="attn_prompt",
    )(sb_bias, q, kb, vb)


CONV_PAD = 32
CONV_TIME = 64


def _conv_prompt_kernel(glu_ref, w_ref, b_ref, g_ref, beta_ref, o_ref, ext_ref, y_ref, sh_ref):
    S = glu_ref.shape[1]
    tt = CONV_TIME
    ext_ref[0:CONV_PAD, :] = jnp.zeros((CONV_PAD, CONV_CH), F32)
    ext_ref[CONV_PAD:CONV_PAD + S, :] = glu_ref[0]
    off = CONV_PAD - CONV_HIST

    def conv_tile(i, _):
        t0 = pl.multiple_of(i * tt, tt)
        for cb in range(CONV_CH // LANES):
            cs = slice(cb * LANES, (cb + 1) * LANES)
            win = ext_ref[pl.ds(t0, tt + CONV_PAD), cs]
            acc = jnp.zeros((tt, LANES), F32) + b_ref[:, cs]
            for res in range(SUBLANES):
                taps = [w for w in range(CONV_WIDTH) if (off + w) % SUBLANES == res]
                span = max(off + w - res for w in taps) + tt
                sh_ref[res, 0:span, :] = win[res:res + span, :]
                for w in taps:
                    base = off + w - res
                    acc = acc + sh_ref[res, base:base + tt, :] * w_ref[w:w + 1, cs]
            y_ref[pl.ds(t0, tt), cs] = acc
        return 0

    lax.fori_loop(0, S // tt, conv_tile, 0)

    def norm_tile(i, _):
        t0 = pl.multiple_of(i * tt, tt)
        acc = y_ref[pl.ds(t0, tt), :]
        mu = jnp.mean(acc, axis=-1, keepdims=True)
        d = acc - mu
        var = jnp.mean(d * d, axis=-1, keepdims=True)
        y = d * lax.rsqrt(var + EPS) * g_ref[...] + beta_ref[...]
        o_ref[0, pl.ds(t0, tt), :] = (y * _sigmoid(y)).astype(o_ref.dtype)
        return 0

    lax.fori_loop(0, S // tt, norm_tile, 0)


def _conv_prompt(glu, w_dw, b_dw, ln_g, ln_b):
    B, S, C = glu.shape
    row = lambda: pl.BlockSpec((1, C), lambda b: (0, 0))
    return pl.pallas_call(
        _conv_prompt_kernel,
        grid=(B,),
        in_specs=[pl.BlockSpec((1, S, C), lambda b: (b, 0, 0)),
                  pl.BlockSpec((CONV_WIDTH, C), lambda b: (0, 0)),
                  row(), row(), row()],
        out_specs=pl.BlockSpec((1, S, C), lambda b: (b, 0, 0)),
        out_shape=jax.ShapeDtypeStruct((B, S, C), BF16),
        scratch_shapes=[pltpu.VMEM((S + CONV_PAD, C), F32), pltpu.VMEM((S, C), F32),
                        pltpu.VMEM((SUBLANES, CONV_TIME + CONV_PAD, LANES), F32)],
        compiler_params=_params("arbitrary"),
        name="conv_prompt",
    )(glu, w_dw, b_dw.reshape(1, C), ln_g.reshape(1, C), ln_b.reshape(1, C))


def _conv_sample_kernel(st_ref, glu_ref, w_ref, b_ref, g_ref, beta_ref, o_ref, nst_ref):
    nq = glu_ref.shape[0]

    def ext(j):
        return st_ref[j] if j < CONV_HIST else glu_ref[j - CONV_HIST]

    for i in range(nq):
        acc = jnp.zeros(st_ref.shape[1:], F32) + b_ref[...]
        for w in range(CONV_WIDTH):
            acc = acc + ext(i + w) * w_ref[w:w + 1, :]
        mu = jnp.mean(acc, axis=-1, keepdims=True)
        d = acc - mu
        var = jnp.mean(d * d, axis=-1, keepdims=True)
        y = d * lax.rsqrt(var + EPS) * g_ref[...] + beta_ref[...]
        o_ref[i] = (y * _sigmoid(y)).astype(o_ref.dtype)
    for j in range(CONV_HIST):
        nst_ref[j] = ext(j + nq)


def _conv_sample(state_t, glu_t, w_dw, b_dw, ln_g, ln_b):
    H, Bs, C = state_t.shape
    nq = glu_t.shape[0]
    full = lambda shape: pl.BlockSpec(shape, lambda i: (0,) * len(shape))
    return pl.pallas_call(
        _conv_sample_kernel,
        grid=(1,),
        in_specs=[full((H, Bs, C)), full((nq, Bs, C)), full((CONV_WIDTH, C)),
                  full((1, C)), full((1, C)), full((1, C))],
        out_specs=[full((nq, Bs, C)), full((H, Bs, C))],
        out_shape=[jax.ShapeDtypeStruct((nq, Bs, C), BF16),
                   jax.ShapeDtypeStruct((H, Bs, C), F32)],
        compiler_params=_params("arbitrary"),
        name="conv_sample",
    )(state_t, glu_t, w_dw, b_dw.reshape(1, C), ln_g.reshape(1, C), ln_b.reshape(1, C))


def _attn_sample_kernel(pt_ref, q_ref, kn_ref, vn_ref, bias_ref, *rest):
    G = PAGES_PER_STEP
    k_refs = rest[:G]
    v_refs = rest[G:2 * G]
    o_ref = rest[2 * G]
    acc_ref, car_ref = rest[2 * G + 1:]
    s = pl.program_id(1)
    nq = q_ref.shape[1]
    rows = nq * N_HEADS

    hrow = lax.broadcasted_iota(jnp.int32, (N_HEADS, D_ATTN), 0)
    hcol = lax.broadcasted_iota(jnp.int32, (N_HEADS, D_ATTN), 1) // HEAD_DIM
    headmask = hrow == hcol
    q = q_ref[0]
    qbd = jnp.concatenate(
        [jnp.where(headmask, q[i:i + 1, :], 0.0) for i in range(nq)], axis=0).astype(BF16)
    bias = bias_ref[...]
    upper = _upper_incl(PAGE_SIZE)

    @pl.when(s == 0)
    def _():
        r = lax.broadcasted_iota(jnp.int32, (rows, PAGE_SIZE), 0) // N_HEADS
        c = lax.broadcasted_iota(jnp.int32, (rows, PAGE_SIZE), 1)
        valid = c < r
        expo, rowsum = _sb_scores(_dot(qbd, kn_ref[0]) + bias, upper, valid)
        acc_ref[...] = _dot_t(_sb_weights(expo, 0.0, valid), vn_ref[0])
        car_ref[...] = rowsum

    car = car_ref[...]
    acc = acc_ref[...]
    kts = [k_refs[j][0].reshape(D_ATTN, PAGE_SIZE).astype(BF16) for j in range(G)]
    zs = [_dot(qbd, kt) + bias for kt in kts]
    scores = [_sb_scores(z, upper, None) for z in zs]
    for j, (expo, rowsum) in enumerate(scores):
        vt = v_refs[j][0].reshape(D_ATTN, PAGE_SIZE).astype(BF16)
        acc = acc + _dot_t(_sb_weights(expo, car, None), vt)
        car = car + rowsum
    acc_ref[...] = acc
    car_ref[...] = car

    @pl.when(s == pl.num_programs(1) - 1)
    def _():
        acc = acc_ref[...]
        out = [jnp.sum(jnp.where(headmask, acc[i * N_HEADS:(i + 1) * N_HEADS, :], 0.0),
                       axis=0, keepdims=True) for i in range(nq)]
        o_ref[0] = jnp.concatenate(out, axis=0).astype(o_ref.dtype)


def _attn_sample(q, knew_t, vnew_t, bias_rows, cache_kt, cache_vt, page_table):
    Bs, nq, _ = q.shape
    n_pages = page_table.shape[1]
    G = PAGES_PER_STEP
    assert n_pages % G == 0
    rows = nq * N_HEADS

    def page_spec(j):
        return pl.BlockSpec(
            (1, N_HEADS, HEAD_DIM, PAGE_SIZE),
            lambda b, s, pt, j=j: (pt[b, n_pages - 1 - (s * G + j)], 0, 0, 0))

    per_b = lambda shape: pl.BlockSpec((1,) + shape, lambda b, s, pt: (b, 0, 0))
    return pl.pallas_call(
        _attn_sample_kernel,
        grid_spec=pltpu.PrefetchScalarGridSpec(
            num_scalar_prefetch=1,
            grid=(Bs, n_pages // G),
            in_specs=[per_b((nq, D_ATTN)), per_b((D_ATTN, PAGE_SIZE)), per_b((D_ATTN, PAGE_SIZE)),
                      pl.BlockSpec((rows, 1), lambda b, s, pt: (0, 0))]
                     + [page_spec(j) for j in range(G)] * 2,
            out_specs=per_b((nq, D_ATTN)),
            scratch_shapes=[pltpu.VMEM((rows, D_ATTN), F32), pltpu.VMEM((rows, 1), F32)]),
        out_shape=jax.ShapeDtypeStruct((Bs, nq, D_ATTN), F32),
        compiler_params=_params("arbitrary", "arbitrary"),
        name="attn_sample",
    )(page_table, q, knew_t, vnew_t, bias_rows, *([cache_kt] * G), *([cache_vt] * G))


def _outproj_kernel(x_ref, a_ref, c_ref, ga_ref, sc_ref, sh_ref, g_ref, w_ref,
                    wr_hi_ref, wr_lo_ref, br_ref, x1_ref, h2_ref, lg_ref):
    mix = _dot(a_ref[0], w_ref[0:D_ATTN, :]) + _dot(c_ref[0], w_ref[D_ATTN:, :])
    x1 = x_ref[0] + ga_ref[0] * mix
    x1_ref[0] = x1
    ms = jnp.mean(x1 * x1, axis=-1, keepdims=True)
    h2 = x1 * lax.rsqrt(ms + EPS) * g_ref[...]
    h2 = h2 * (1.0 + sc_ref[0]) + sh_ref[0]
    _rows_to_tiles(h2_ref.at[0], h2)
    hi, lo = _split_bf16(h2)
    lg_ref[0] = (_dot(hi, wr_hi_ref[...]) + _dot(lo, wr_hi_ref[...])
                 + _dot(hi, wr_lo_ref[...]) + br_ref[...])


def _outproj(x, attn, conv, ga, sc, sh, g_norm, w_out_bf, wr_hi, wr_lo, br_pad):
    G, R, D = x.shape
    ts = min(ROW_TILE, R)
    tok = lambda n: pl.BlockSpec((1, ts, n), lambda g, t: (g, t, 0))
    const = lambda shape: pl.BlockSpec(shape, lambda g, t: (0, 0))
    return pl.pallas_call(
        _outproj_kernel,
        grid=(G, R // ts),
        in_specs=[tok(D), tok(D_ATTN), tok(CONV_CH), _mod_spec(ga, ts), _mod_spec(sc, ts),
                  _mod_spec(sh, ts), const((1, D)), const((D, D)),
                  const((D, LANES)), const((D, LANES)), const((1, LANES))],
        out_specs=[tok(D), pl.BlockSpec((1, ts * SUBLANES, LANES), lambda g, t: (g, t, 0)), tok(LANES)],
        out_shape=[jax.ShapeDtypeStruct((G, R, D), F32),
                   jax.ShapeDtypeStruct((G, R * SUBLANES, LANES), F32),
                   jax.ShapeDtypeStruct((G, R, LANES), F32)],
        compiler_params=_params("arbitrary", "arbitrary"),
        name="outproj",
    )(x, attn, conv, ga, sc, sh, g_norm.reshape(1, D), w_out_bf, wr_hi, wr_lo, br_pad)


def _route_kernel(lg_ref, dest_ref, gate_ref, cnt_ref, counts, offs, run):
    ph = pl.program_id(0)
    t = pl.program_id(1)
    tr = lg_ref.shape[0]
    lane = lax.broadcasted_iota(jnp.int32, (tr, LANES), 1).astype(F32)
    l = lg_ref[...]
    onehots, vals = [], []
    for _ in range(TOP_K):
        mx = jnp.max(l, axis=-1, keepdims=True)
        idx = jnp.min(jnp.where(l == mx, lane, float(LANES)), axis=-1, keepdims=True)
        oh = lane == idx
        onehots.append(oh)
        vals.append(mx)
        l = jnp.where(oh, NEG_BIG, l)
    total = jnp.zeros((tr, LANES), F32)
    for oh in onehots:
        total = total + jnp.where(oh, 1.0, 0.0)
    colsum = jnp.sum(total, axis=0, keepdims=True)

    @pl.when(jnp.logical_and(ph == 0, t == 0))
    def _():
        counts[...] = jnp.zeros_like(counts)

    @pl.when(ph == 0)
    def _():
        counts[...] += colsum

    @pl.when(jnp.logical_and(ph == 1, t == 0))
    def _():
        c = jnp.broadcast_to(counts[...], (8, LANES))
        c1 = c.astype(BF16)
        r1 = c - c1.astype(F32)
        c2 = r1.astype(BF16)
        c3 = (r1 - c2.astype(F32)).astype(BF16)
        rr = lax.broadcasted_iota(jnp.int32, (LANES, LANES), 0)
        cc = lax.broadcasted_iota(jnp.int32, (LANES, LANES), 1)
        before = jnp.where(rr < cc, 1.0, 0.0).astype(BF16)
        o = _dot(c1, before) + _dot(c2, before) + _dot(c3, before)
        offs[...] = o[0:1, :]
        run[...] = jnp.zeros_like(run)
        cnt_ref[...] = counts[...]

    @pl.when(ph == 1)
    def _():
        rr = lax.broadcasted_iota(jnp.int32, (tr, tr), 0)
        cc = lax.broadcasted_iota(jnp.int32, (tr, tr), 1)
        earlier = jnp.where(cc < rr, 1.0, 0.0).astype(BF16)
        pos = _dot(earlier, total.astype(BF16)) + (offs[...] + run[...])
        kcol = lax.broadcasted_iota(jnp.int32, (tr, LANES), 1)
        dest = jnp.zeros((tr, LANES), F32)
        gate = jnp.zeros((tr, LANES), F32)
        es = [jnp.exp(v - vals[0]) for v in vals]
        den = es[0] + es[1] + es[2] + es[3]
        for k in range(TOP_K):
            dk = jnp.sum(jnp.where(onehots[k], pos, 0.0), axis=-1, keepdims=True)
            dest = jnp.where(kcol == k, dk, dest)
            gate = jnp.where(kcol == k, es[k] / den, gate)
        dest_ref[...] = dest[:, 0:TOP_K].astype(jnp.int32)
        gate_ref[...] = gate[:, 0:TOP_K]
        run[...] += colsum


def _route(logits):
    T = logits.shape[0]
    tr = min(ROUTE_TILE, T)
    return pl.pallas_call(
        _route_kernel,
        grid=(2, T // tr),
        in_specs=[pl.BlockSpec((tr, LANES), lambda ph, t: (t, 0))],
        out_specs=[pl.BlockSpec((tr, TOP_K), lambda ph, t: (ph * t, 0)),
                   pl.BlockSpec((tr, TOP_K), lambda ph, t: (ph * t, 0)),
                   pl.BlockSpec((1, LANES), lambda ph, t: (0, 0))],
        out_shape=[jax.ShapeDtypeStruct((T, TOP_K), jnp.int32),
                   jax.ShapeDtypeStruct((T, TOP_K), F32),
                   jax.ShapeDtypeStruct((1, LANES), F32)],
        scratch_shapes=[pltpu.VMEM((1, LANES), F32)] * 3,
        compiler_params=_params("arbitrary", "arbitrary"),
        name="route",
    )(logits)


def _row_copy(src, dst, sem):
    return pltpu.make_async_copy(src, dst, sem)


def _dispatch_kernel(dest_ref, h_ref, xs_ref, sem):
    td = h_ref.shape[0] // SUBLANES

    def issue(t, _):
        for k in range(TOP_K):
            d = dest_ref[0, 0, t * TOP_K + k]
            _row_copy(_tile_of(h_ref, t), _tile_of(xs_ref, d), sem).start(priority=k % 2)
        return 0

    lax.fori_loop(0, td, issue, 0)

    def drain(t, _):
        for k in range(TOP_K):
            _row_copy(_tile_of(h_ref, 0), _tile_of(xs_ref, 0), sem).wait()
        return 0

    lax.fori_loop(0, td, drain, 0)


def _dispatch(h2, dest):
    T = h2.shape[0] // SUBLANES
    td = min(ROUTE_TILE, T)
    dest3 = dest.reshape(T // td, 1, td * TOP_K)
    return pl.pallas_call(
        _dispatch_kernel,
        grid=(T // td,),
        in_specs=[pl.BlockSpec((1, 1, td * TOP_K), lambda i: (i, 0, 0), memory_space=pltpu.SMEM),
                  pl.BlockSpec((td * SUBLANES, LANES), lambda i: (i, 0))],
        out_specs=pl.BlockSpec(memory_space=pl.ANY),
        out_shape=jax.ShapeDtypeStruct((T * TOP_K * SUBLANES, LANES), h2.dtype),
        scratch_shapes=[pltpu.SemaphoreType.DMA(())],
        compiler_params=_params("arbitrary"),
        name="dispatch",
    )(dest3, h2)


MOE_FF_CHUNK = 256


def _moe_kernel(blk_ref, exp_ref, lo_ref, hi_ref, first_ref, n_ref,
                x_ref, wgu_ref, bgu_ref, wd_ref, bd_ref, o_ref):
    w = pl.program_id(0)
    bm = x_ref.shape[0] // SUBLANES

    @pl.when(w < n_ref[0])
    def _():
        x = _tiles_to_rows(x_ref).astype(BF16)
        res = jnp.zeros((bm, D_MODEL), F32) + bd_ref[0]
        for n in range(D_FF // MOE_FF_CHUNK):
            c0 = n * MOE_FF_CHUNK
            c1 = c0 + MOE_FF_CHUNK
            g = _dot(x, wgu_ref[0, :, c0:c1]) + bgu_ref[0, :, c0:c1]
            u = _dot(x, wgu_ref[0, :, D_FF + c0:D_FF + c1]) + bgu_ref[0, :, D_FF + c0:D_FF + c1]
            g = jnp.minimum(g, SWIGLU_LIMIT)
            u = jnp.clip(u, -SWIGLU_LIMIT, SWIGLU_LIMIT)
            act = (u + 1.0) * (g * _sigmoid(SWIGLU_ALPHA * g))
            res = res + _dot(act.astype(BF16), wd_ref[0, c0:c1, :])

        @pl.when(first_ref[w] == 1)
        def _():
            _rows_to_tiles(o_ref, res)

        @pl.when(first_ref[w] == 0)
        def _():
            row = blk_ref[w] * bm + lax.broadcasted_iota(jnp.int32, (bm, 1), 0)
            mine = jnp.logical_and(row >= lo_ref[w], row < hi_ref[w])
            _rows_to_tiles(o_ref, jnp.where(mine, res, _tiles_to_rows(o_ref)))


def _moe_schedule(counts, n_rows, bm):
    c = counts.astype(jnp.int32)
    end = jnp.cumsum(c)
    start = end - c
    first_blk = start // bm
    last_blk = jnp.where(c > 0, (end - 1) // bm, first_blk - 1)
    n_items_e = last_blk - first_blk + 1
    item_end = jnp.cumsum(n_items_e)
    item_start = item_end - n_items_e
    n_items = item_end[-1]
    w_max = n_rows // bm + N_EXPERTS - 1
    w = jnp.minimum(jnp.arange(w_max, dtype=jnp.int32), n_items - 1)
    e = jnp.sum((item_end[None, :] <= w[:, None]).astype(jnp.int32), axis=1)
    blk = first_blk[e] + (w - item_start[e])
    prev = jnp.concatenate([jnp.full((1,), -1, jnp.int32), blk[:-1]])
    first = (blk != prev).astype(jnp.int32)
    return blk, e, start[e], end[e], first, n_items.reshape(1)


def _moe(xs, counts, wgu_bf, bgu, wd_bf, bd):
    R = xs.shape[0] // SUBLANES
    D = D_MODEL
    bm = min(MOE_TILE, R)
    blk, e, lo, hi, first, n_items = _moe_schedule(counts, R, bm)
    w_max = blk.shape[0]
    return pl.pallas_call(
        _moe_kernel,
        grid_spec=pltpu.PrefetchScalarGridSpec(
            num_scalar_prefetch=6,
            grid=(w_max,),
            in_specs=[pl.BlockSpec((bm * SUBLANES, LANES), lambda w, blk, e, *_: (blk[w], 0)),
                      pl.BlockSpec((1, D, 2 * D_FF), lambda w, blk, e, *_: (e[w], 0, 0)),
                      pl.BlockSpec((1, 1, 2 * D_FF), lambda w, blk, e, *_: (e[w], 0, 0)),
                      pl.BlockSpec((1, D_FF, D), lambda w, blk, e, *_: (e[w], 0, 0)),
                      pl.BlockSpec((1, 1, D), lambda w, blk, e, *_: (e[w], 0, 0))],
            out_specs=pl.BlockSpec((bm * SUBLANES, LANES), lambda w, blk, e, *_: (blk[w], 0))),
        out_shape=jax.ShapeDtypeStruct((R * SUBLANES, LANES), F32),
        compiler_params=_params("arbitrary"),
        name="moe",
    )(blk, e, lo, hi, first, n_items, xs, wgu_bf,
      bgu.reshape(N_EXPERTS, 1, 2 * D_FF), wd_bf, bd.reshape(N_EXPERTS, 1, D))


def _combine_kernel(dest_ref, gate_ref, x1_ref, ga_ref, g_ref, ys_ref, o_ref, buf, sem):
    tc = x1_ref.shape[1]

    def issue(t, _):
        for k in range(TOP_K):
            d = dest_ref[0, 0, 0, t * TOP_K + k]
            _row_copy(_tile_of(ys_ref, d), _tile_of(buf.at[k], t), sem).start(priority=k % 2)
        return 0

    lax.fori_loop(0, tc, issue, 0)

    def drain(t, _):
        for k in range(TOP_K):
            _row_copy(_tile_of(ys_ref, 0), _tile_of(buf.at[0], 0), sem).wait()
        return 0

    lax.fori_loop(0, tc, drain, 0)
    gate = gate_ref[0]
    ff = jnp.zeros((tc, D_MODEL), F32)
    for k in range(TOP_K):
        ff = ff + _tiles_to_rows(buf.at[k]) * gate[:, k:k + 1]
    x2 = x1_ref[0] + ga_ref[0] * ff
    ms = jnp.mean(x2 * x2, axis=-1, keepdims=True)
    o_ref[0] = x2 * lax.rsqrt(ms + EPS) * g_ref[...]


def _combine(ys, dest, gates, x1, ga, g_final):
    G, R, D = x1.shape
    tc = min(COMBINE_TILE, R)
    nt = R // tc
    dest4 = dest.reshape(G, nt, 1, tc * TOP_K)
    gates3 = gates.reshape(G, R, TOP_K)
    return pl.pallas_call(
        _combine_kernel,
        grid=(G, nt),
        in_specs=[pl.BlockSpec((1, 1, 1, tc * TOP_K), lambda g, t: (g, t, 0, 0),
                               memory_space=pltpu.SMEM),
                  pl.BlockSpec((1, tc, TOP_K), lambda g, t: (g, t, 0)),
                  pl.BlockSpec((1, tc, D), lambda g, t: (g, t, 0)),
                  _mod_spec(ga, tc),
                  pl.BlockSpec((1, D), lambda g, t: (0, 0)),
                  pl.BlockSpec(memory_space=pl.ANY)],
        out_specs=pl.BlockSpec((1, tc, D), lambda g, t: (g, t, 0)),
        out_shape=jax.ShapeDtypeStruct((G, R, D), F32),
        scratch_shapes=[pltpu.VMEM((TOP_K, tc * SUBLANES, LANES), F32), pltpu.SemaphoreType.DMA(())],
        compiler_params=_params("arbitrary", "arbitrary"),
        name="combine",
    )(dest4, gates3, x1, ga, g_final.reshape(1, D), ys)


def _ffn(x, attn, conv, mods, wts, g_final):
    ga1, sc2, sh2, ga2 = mods
    G, R, D = x.shape
    x1, h2, logits = _outproj(x, attn, conv, ga1, sc2, sh2, wts["g_ffn"], wts["w_out"],
                              wts["wr_hi"], wts["wr_lo"], wts["br"])
    T = G * R
    dest, gates, counts = _route(logits.reshape(T, LANES))
    xs = _dispatch(h2.reshape(T * SUBLANES, LANES), dest)
    ys = _moe(xs, counts[0, :N_EXPERTS], wts["wgu"], wts["bgu"], wts["wd"], wts["bd"])
    return _combine(ys, dest, gates, x1, ga2, g_final)


def kernel(x_prompt, x_sample, c_prompt, c_sample, cache_k, cache_v, state_conv, page_table,
           g_attn_norm, g_ffn_norm, w_ada, b_ada, w_in, sb_bias, w_dw, b_dw, ln_conv_g, ln_conv_b,
           w_out, w_router, b_router, w_gate_up, b_gate_up, w_down, b_down, g_final):
    assert w_ada.shape[0] == 1, "one trunk layer"
    B, S, D = x_prompt.shape
    Bs, nq, _ = x_sample.shape
    n_pool = cache_k.shape[1]

    wr = jnp.pad(w_router[0], ((0, 0), (0, LANES - N_EXPERTS)))
    wr_hi = wr.astype(BF16)
    wts = dict(
        g_ffn=g_ffn_norm[0], w_out=w_out[0].astype(BF16),
        wr_hi=wr_hi, wr_lo=(wr - wr_hi.astype(F32)).astype(BF16),
        br=jnp.pad(b_router[0], (0, LANES - N_EXPERTS), constant_values=NEG_BIG).reshape(1, LANES),
        wgu=w_gate_up[0].astype(BF16), bgu=b_gate_up[0],
        wd=w_down[0].astype(BF16), bd=b_down[0])
    w_in_bf = w_in[0].astype(BF16)

    ada = _ada(jnp.concatenate([c_prompt, c_sample], axis=0), w_ada[0], b_ada[0])
    ada_p = ada[:B].reshape(B, 1, 6, D)
    ada_s = jnp.broadcast_to(ada[B:].reshape(Bs, 1, 6, D), (Bs, nq, 6, D)).reshape(1, Bs * nq, 6, D)
    mod_p = [ada_p[:, :, i, :] for i in range(6)]
    mod_s = [ada_s[:, :, i, :] for i in range(6)]

    q, k, v, kb, vb, glu = _inproj(x_prompt, mod_p[1], mod_p[0], g_attn_norm[0], w_in_bf)
    attn = _attn_prompt(q, kb, vb, sb_bias[0])
    conv = _conv_prompt(glu, w_dw[0], b_dw[0], ln_conv_g[0], ln_conv_b[0])
    y_prompt = _ffn(x_prompt, attn, conv, (mod_p[2], mod_p[4], mod_p[3], mod_p[5]), wts, g_final)
    k_prompt = k.reshape(1, B, S, N_HEADS, HEAD_DIM)
    v_prompt = v.reshape(1, B, S, N_HEADS, HEAD_DIM)
    conv_prompt = glu[:, S - CONV_HIST:, :][None]

    xs = x_sample.reshape(1, Bs * nq, D)
    q_s, k_s, v_s, kb_s, vb_s, glu_s = _inproj(xs, mod_s[1], mod_s[0], g_attn_norm[0], w_in_bf)

    def new_page(a):
        a = jnp.transpose(a.reshape(Bs, nq, D_ATTN), (0, 2, 1))
        return jnp.pad(a, ((0, 0), (0, 0), (0, PAGE_SIZE - nq)))

    to_pages = lambda c: jnp.transpose(c[0], (0, 2, 3, 1))
    bias_rows = jnp.tile(sb_bias[0], nq).reshape(nq * N_HEADS, 1)
    attn_s = _attn_sample(q_s.reshape(Bs, nq, D_ATTN).astype(F32), new_page(kb_s), new_page(vb_s), bias_rows,
                          to_pages(cache_k), to_pages(cache_v), page_table)
    state_t = jnp.transpose(state_conv[0], (1, 0, 2))
    glu_t = jnp.transpose(glu_s.reshape(Bs, nq, CONV_CH), (1, 0, 2))
    conv_t, nstate_t = _conv_sample(state_t, glu_t, w_dw[0], b_dw[0], ln_conv_g[0], ln_conv_b[0])
    conv_s = jnp.transpose(conv_t, (1, 0, 2)).reshape(1, Bs * nq, CONV_CH)
    y_s = _ffn(xs, attn_s.reshape(1, Bs * nq, D_ATTN).astype(BF16), conv_s,
               (mod_s[2], mod_s[4], mod_s[3], mod_s[5]), wts, g_final)
    y_sample = y_s.reshape(Bs, nq, D)
    k_sample = k_s.reshape(1, Bs, nq, N_HEADS, HEAD_DIM)
    v_sample = v_s.reshape(1, Bs, nq, N_HEADS, HEAD_DIM)
    conv_sample = jnp.transpose(nstate_t, (1, 0, 2))[None]

    return (y_prompt, y_sample, k_prompt, v_prompt, conv_prompt, k_sample, v_sample, conv_sample)
```

```python
import functools

import jax
import jax.numpy as jnp
from jax import lax
from jax.experimental import pallas as pl
from jax.experimental.pallas import tpu as pltpu

F32 = jnp.float32
BF16 = jnp.bfloat16

D_MODEL = 1024
N_HEADS = 8
HEAD_DIM = 64
D_ATTN = N_HEADS * HEAD_DIM
CONV_CH = D_MODEL - D_ATTN
CONV_WIDTH = 31
CONV_HIST = CONV_WIDTH - 1
D_IN = 3 * D_ATTN + 2 * CONV_CH
N_EXPERTS = 32
TOP_K = 4
D_FF = D_MODEL
SWIGLU_LIMIT = 7.0
SWIGLU_ALPHA = 1.702
EPS = 1e-5
PAGE_SIZE = 128

LANES = 128
SUBLANES = 8
V7X_VMEM_BYTES = 64 * 1024 * 1024
VMEM_LIMIT = 48 * 1024 * 1024

ROW_TILE = 512
ATTN_TILE = 256
ATTN_TILES_PER_ITER = 2
CONV_TILE = 256
ROUTE_TILE = 512
MOE_TILE = 512
COMBINE_TILE = 256
PAGES_PER_STEP = 16
NEG_BIG = -1e30


def _params(*sem):
    return pltpu.CompilerParams(dimension_semantics=sem, vmem_limit_bytes=VMEM_LIMIT)


def _sigmoid(x):
    return 1.0 / (1.0 + jnp.exp(-x))


def _log_sigmoid_neg(z):
    return -(jnp.maximum(z, 0.0) + jnp.log(1.0 + jnp.exp(-jnp.abs(z))))


def _split_bf16(x):
    hi = x.astype(BF16)
    lo = (x - hi.astype(F32)).astype(BF16)
    return hi, lo


def _upper_incl(n):
    r = lax.broadcasted_iota(jnp.int32, (n, n), 0)
    c = lax.broadcasted_iota(jnp.int32, (n, n), 1)
    return jnp.where(r >= c, 1.0, 0.0).astype(BF16)


def _rows_to_tiles(ref_view, x):
    rows = x.shape[0]
    for s in range(SUBLANES):
        ref_view[pl.ds(s, rows, stride=SUBLANES), :] = x[:, s * LANES:(s + 1) * LANES]


def _tiles_to_rows(ref_view):
    rows = ref_view.shape[0] // SUBLANES
    return jnp.concatenate(
        [ref_view[pl.ds(s, rows, stride=SUBLANES), :] for s in range(SUBLANES)], axis=1)


def _tile_of(ref, row):
    return ref.at[pl.ds(pl.multiple_of(row * SUBLANES, SUBLANES), SUBLANES), :]


def _dot(a, b):
    return jnp.dot(a, b, preferred_element_type=F32)


def _dot_t(a, b):
    return lax.dot_general(a, b, (((1,), (1,)), ((), ())), preferred_element_type=F32)


def _ada_kernel(c_ref, w_ref, b_ref, o_ref):
    c = c_ref[...]
    s = (c * _sigmoid(c)).astype(BF16)
    o_ref[...] = _dot(s, w_ref[...].astype(BF16)) + b_ref[...]


def _ada(c_all, w_ada, b_ada):
    n, d = c_all.shape
    nout = w_ada.shape[1]
    tn = 1024
    return pl.pallas_call(
        _ada_kernel,
        grid=(nout // tn,),
        in_specs=[pl.BlockSpec((n, d), lambda j: (0, 0)),
                  pl.BlockSpec((d, tn), lambda j: (0, j)),
                  pl.BlockSpec((1, tn), lambda j: (0, j))],
        out_specs=pl.BlockSpec((n, tn), lambda j: (0, j)),
        out_shape=jax.ShapeDtypeStruct((n, nout), F32),
        compiler_params=_params("arbitrary"),
        name="ada",
    )(c_all, w_ada, b_ada.reshape(1, nout))


def _inproj_kernel(x_ref, sc_ref, sh_ref, g_ref, w_ref,
                   q_ref, k_ref, v_ref, kb_ref, vb_ref, glu_ref):
    x = x_ref[0]
    ms = jnp.mean(x * x, axis=-1, keepdims=True)
    h = x * lax.rsqrt(ms + EPS) * g_ref[...]
    h = (h * (1.0 + sc_ref[0]) + sh_ref[0]).astype(BF16)
    c = D_ATTN
    q_ref[0] = (_dot(h, w_ref[:, 0:c]) * (HEAD_DIM ** -0.5)).astype(BF16)
    k = _dot(h, w_ref[:, c:2 * c])
    k_ref[0] = k
    kb_ref[0] = k.astype(BF16)
    v = _dot(h, w_ref[:, 2 * c:3 * c])
    v_ref[0] = v
    vb_ref[0] = v.astype(BF16)
    a = _dot(h, w_ref[:, 3 * c:3 * c + CONV_CH])
    g = _dot(h, w_ref[:, 3 * c + CONV_CH:])
    glu_ref[0] = a * _sigmoid(g)


def _mod_spec(mod, ts):
    if mod.shape[1] == 1:
        return pl.BlockSpec((1, 1, D_MODEL), lambda g, t: (g, 0, 0))
    return pl.BlockSpec((1, ts, D_MODEL), lambda g, t: (g, t, 0))


def _inproj(x, sc, sh, g_norm, w_in_bf):
    G, R, D = x.shape
    ts = min(ROW_TILE, R)
    tok = lambda n: pl.BlockSpec((1, ts, n), lambda g, t: (g, t, 0))
    shp = lambda n, dt: jax.ShapeDtypeStruct((G, R, n), dt)
    return pl.pallas_call(
        _inproj_kernel,
        grid=(G, R // ts),
        in_specs=[tok(D), _mod_spec(sc, ts), _mod_spec(sh, ts),
                  pl.BlockSpec((1, D), lambda g, t: (0, 0)),
                  pl.BlockSpec((D, D_IN), lambda g, t: (0, 0))],
        out_specs=[tok(D_ATTN)] * 5 + [tok(CONV_CH)],
        out_shape=[shp(D_ATTN, BF16), shp(D_ATTN, F32), shp(D_ATTN, F32),
                   shp(D_ATTN, BF16), shp(D_ATTN, BF16), shp(CONV_CH, F32)],
        compiler_params=_params("arbitrary", "arbitrary"),
        name="inproj",
    )(x, sc, sh, g_norm.reshape(1, D), w_in_bf)


def _sb_scores(z, upper_incl, valid):
    m = _log_sigmoid_neg(z)
    if valid is not None:
        m = jnp.where(valid, m, 0.0)
    suffix = _dot(m.astype(BF16), upper_incl)
    return z + suffix, suffix[:, 0:1]


def _sb_weights(expo, carry, valid):
    a = jnp.exp(expo + carry)
    if valid is not None:
        a = jnp.where(valid, a, 0.0)
    return a.astype(BF16)


def _attn_prompt_kernel(bias_ref, q_ref, k_ref, v_ref, o_ref, acc_e, acc_o, car_e, car_o):
    t = ATTN_TILE
    p = pl.program_id(1)
    qi = pl.program_id(2)
    q2 = q_ref[0]
    lane = lax.broadcasted_iota(jnp.int32, (1, LANES), 1)
    even = lane < HEAD_DIM
    q_e = jnp.where(even, q2, jnp.zeros_like(q2))
    q_o = jnp.where(even, jnp.zeros_like(q2), q2)
    b_e = bias_ref[2 * p]
    b_o = bias_ref[2 * p + 1]
    upper = _upper_incl(t)

    def tiles(kbs, valids):
        ce, co = car_e[...], car_o[...]
        oe = oo = None
        for kb, valid in zip(kbs, valids):
            st = pl.multiple_of(kb * t, t)
            kblk = k_ref[0, pl.ds(st, t), :]
            vblk = v_ref[0, pl.ds(st, t), :]
            xe, se = _sb_scores(_dot_t(q_e, kblk) + b_e, upper, valid)
            xo, so = _sb_scores(_dot_t(q_o, kblk) + b_o, upper, valid)
            de = _dot(_sb_weights(xe, ce, valid), vblk)
            do = _dot(_sb_weights(xo, co, valid), vblk)
            oe = de if oe is None else oe + de
            oo = do if oo is None else oo + do
            ce = ce + se
            co = co + so
        acc_e[...] += oe
        acc_o[...] += oo
        car_e[...] = ce
        car_o[...] = co

    acc_e[...] = jnp.zeros_like(acc_e)
    acc_o[...] = jnp.zeros_like(acc_o)
    car_e[...] = jnp.zeros_like(car_e)
    car_o[...] = jnp.zeros_like(car_o)

    r = lax.broadcasted_iota(jnp.int32, (t, t), 0)
    c = lax.broadcasted_iota(jnp.int32, (t, t), 1)
    diag = c < r

    @pl.when(qi == 0)
    def _():
        tiles([0], [diag])

    @pl.when(qi > 0)
    def _():
        tiles([qi, qi - 1], [diag, None])

    n = ATTN_TILES_PER_ITER
    rest = jnp.maximum(qi - 1, 0)

    def body(i, _):
        kb = rest - 1 - n * i
        tiles([kb - j for j in range(n)], [None] * n)
        return 0

    lax.fori_loop(0, rest // n, body, 0)

    def tail(i, _):
        tiles([rest % n - 1 - i], [None])
        return 0

    lax.fori_loop(0, rest % n, tail, 0)

    o_ref[0] = jnp.where(even, acc_e[...], acc_o[...]).astype(o_ref.dtype)


def _attn_prompt(q, kb, vb, sb_bias):
    B, S, _ = q.shape
    t = ATTN_TILE
    npair = N_HEADS // 2
    return pl.pallas_call(
        _attn_prompt_kernel,
        grid_spec=pltpu.PrefetchScalarGridSpec(
            num_scalar_prefetch=0,
            grid=(B, npair, S // t),
            in_specs=[pl.BlockSpec(memory_space=pltpu.SMEM),
                      pl.BlockSpec((1, t, LANES), lambda b, p, i: (b, i, p)),
                      pl.BlockSpec((1, S, LANES), lambda b, p, i: (b, 0, p)),
                      pl.BlockSpec((1, S, LANES), lambda b, p, i: (b, 0, p))],
            out_specs=pl.BlockSpec((1, t, LANES), lambda b, p, i: (b, i, p)),
            scratch_shapes=[pltpu.VMEM((t, LANES), F32), pltpu.VMEM((t, LANES), F32),
                            pltpu.VMEM((t, 1), F32), pltpu.VMEM((t, 1), F32)]),
        out_shape=jax.ShapeDtypeStruct((B, S, D_ATTN), BF16),
        compiler_params=_params("arbitrary", "arbitrary", "arbitrary"),
        name="attn_prompt",
    )(sb_bias, q, kb, vb)


CONV_PAD = 32
CONV_TIME = 64


def _conv_prompt_kernel(glu_ref, w_ref, b_ref, g_ref, beta_ref, o_ref, ext_ref, y_ref, sh_ref):
    S = glu_ref.shape[1]
    tt = CONV_TIME
    ext_ref[0:CONV_PAD, :] = jnp.zeros((CONV_PAD, CONV_CH), F32)
    ext_ref[CONV_PAD:CONV_PAD + S, :] = glu_ref[0]
    off = CONV_PAD - CONV_HIST

    def conv_tile(i, _):
        t0 = pl.multiple_of(i * tt, tt)
        for cb in range(CONV_CH // LANES):
            cs = slice(cb * LANES, (cb + 1) * LANES)
            win = ext_ref[pl.ds(t0, tt + CONV_PAD), cs]
            acc = jnp.zeros((tt, LANES), F32) + b_ref[:, cs]
            for res in range(SUBLANES):
                taps = [w for w in range(CONV_WIDTH) if (off + w) % SUBLANES == res]
                span = max(off + w - res for w in taps) + tt
                sh_ref[res, 0:span, :] = win[res:res + span, :]
                for w in taps:
                    base = off + w - res
                    acc = acc + sh_ref[res, base:base + tt, :] * w_ref[w:w + 1, cs]
            y_ref[pl.ds(t0, tt), cs] = acc
        return 0

    lax.fori_loop(0, S // tt, conv_tile, 0)

    def norm_tile(i, _):
        t0 = pl.multiple_of(i * tt, tt)
        acc = y_ref[pl.ds(t0, tt), :]
        mu = jnp.mean(acc, axis=-1, keepdims=True)
        d = acc - mu
        var = jnp.mean(d * d, axis=-1, keepdims=True)
        y = d * lax.rsqrt(var + EPS) * g_ref[...] + beta_ref[...]
        o_ref[0, pl.ds(t0, tt), :] = (y * _sigmoid(y)).astype(o_ref.dtype)
        return 0

    lax.fori_loop(0, S // tt, norm_tile, 0)


def _conv_prompt(glu, w_dw, b_dw, ln_g, ln_b):
    B, S, C = glu.shape
    row = lambda: pl.BlockSpec((1, C), lambda b: (0, 0))
    return pl.pallas_call(
        _conv_prompt_kernel,
        grid=(B,),
        in_specs=[pl.BlockSpec((1, S, C), lambda b: (b, 0, 0)),
                  pl.BlockSpec((CONV_WIDTH, C), lambda b: (0, 0)),
                  row(), row(), row()],
        out_specs=pl.BlockSpec((1, S, C), lambda b: (b, 0, 0)),
        out_shape=jax.ShapeDtypeStruct((B, S, C), BF16),
        scratch_shapes=[pltpu.VMEM((S + CONV_PAD, C), F32), pltpu.VMEM((S, C), F32),
                        pltpu.VMEM((SUBLANES, CONV_TIME + CONV_PAD, LANES), F32)],
        compiler_params=_params("arbitrary"),
        name="conv_prompt",
    )(glu, w_dw, b_dw.reshape(1, C), ln_g.reshape(1, C), ln_b.reshape(1, C))


def _conv_sample_kernel(st_ref, glu_ref, w_ref, b_ref, g_ref, beta_ref, o_ref, nst_ref):
    nq = glu_ref.shape[0]

    def ext(j):
        return st_ref[j] if j < CONV_HIST else glu_ref[j - CONV_HIST]

    for i in range(nq):
        acc = jnp.zeros(st_ref.shape[1:], F32) + b_ref[...]
        for w in range(CONV_WIDTH):
            acc = acc + ext(i + w) * w_ref[w:w + 1, :]
        mu = jnp.mean(acc, axis=-1, keepdims=True)
        d = acc - mu
        var = jnp.mean(d * d, axis=-1, keepdims=True)
        y = d * lax.rsqrt(var + EPS) * g_ref[...] + beta_ref[...]
        o_ref[i] = (y * _sigmoid(y)).astype(o_ref.dtype)
    for j in range(CONV_HIST):
        nst_ref[j] = ext(j + nq)


def _conv_sample(state_t, glu_t, w_dw, b_dw, ln_g, ln_b):
    H, Bs, C = state_t.shape
    nq = glu_t.shape[0]
    full = lambda shape: pl.BlockSpec(shape, lambda i: (0,) * len(shape))
    return pl.pallas_call(
        _conv_sample_kernel,
        grid=(1,),
        in_specs=[full((H, Bs, C)), full((nq, Bs, C)), full((CONV_WIDTH, C)),
                  full((1, C)), full((1, C)), full((1, C))],
        out_specs=[full((nq, Bs, C)), full((H, Bs, C))],
        out_shape=[jax.ShapeDtypeStruct((nq, Bs, C), BF16),
                   jax.ShapeDtypeStruct((H, Bs, C), F32)],
        compiler_params=_params("arbitrary"),
        name="conv_sample",
    )(state_t, glu_t, w_dw, b_dw.reshape(1, C), ln_g.reshape(1, C), ln_b.reshape(1, C))


def _attn_sample_kernel(pt_ref, q_ref, kn_ref, vn_ref, bias_ref, *rest):
    G = PAGES_PER_STEP
    k_refs = rest[:G]
    v_refs = rest[G:2 * G]
    o_ref = rest[2 * G]
    acc_ref, car_ref = rest[2 * G + 1:]
    s = pl.program_id(1)
    nq = q_ref.shape[1]
    rows = nq * N_HEADS

    hrow = lax.broadcasted_iota(jnp.int32, (N_HEADS, D_ATTN), 0)
    hcol = lax.broadcasted_iota(jnp.int32, (N_HEADS, D_ATTN), 1) // HEAD_DIM
    headmask = hrow == hcol
    q = q_ref[0]
    qbd = jnp.concatenate(
        [jnp.where(headmask, q[i:i + 1, :], 0.0) for i in range(nq)], axis=0).astype(BF16)
    bias = bias_ref[...]
    upper = _upper_incl(PAGE_SIZE)

    @pl.when(s == 0)
    def _():
        r = lax.broadcasted_iota(jnp.int32, (rows, PAGE_SIZE), 0) // N_HEADS
        c = lax.broadcasted_iota(jnp.int32, (rows, PAGE_SIZE), 1)
        valid = c < r
        expo, rowsum = _sb_scores(_dot(qbd, kn_ref[0]) + bias, upper, valid)
        acc_ref[...] = _dot_t(_sb_weights(expo, 0.0, valid), vn_ref[0])
        car_ref[...] = rowsum

    car = car_ref[...]
    acc = acc_ref[...]
    kts = [k_refs[j][0].reshape(D_ATTN, PAGE_SIZE).astype(BF16) for j in range(G)]
    zs = [_dot(qbd, kt) + bias for kt in kts]
    scores = [_sb_scores(z, upper, None) for z in zs]
    for j, (expo, rowsum) in enumerate(scores):
        vt = v_refs[j][0].reshape(D_ATTN, PAGE_SIZE).astype(BF16)
        acc = acc + _dot_t(_sb_weights(expo, car, None), vt)
        car = car + rowsum
    acc_ref[...] = acc
    car_ref[...] = car

    @pl.when(s == pl.num_programs(1) - 1)
    def _():
        acc = acc_ref[...]
        out = [jnp.sum(jnp.where(headmask, acc[i * N_HEADS:(i + 1) * N_HEADS, :], 0.0),
                       axis=0, keepdims=True) for i in range(nq)]
        o_ref[0] = jnp.concatenate(out, axis=0).astype(o_ref.dtype)


def _attn_sample(q, knew_t, vnew_t, bias_rows, cache_kt, cache_vt, page_table):
    Bs, nq, _ = q.shape
    n_pages = page_table.shape[1]
    G = PAGES_PER_STEP
    assert n_pages % G == 0
    rows = nq * N_HEADS

    def page_spec(j):
        return pl.BlockSpec(
            (1, N_HEADS, HEAD_DIM, PAGE_SIZE),
            lambda b, s, pt, j=j: (pt[b, n_pages - 1 - (s * G + j)], 0, 0, 0))

    per_b = lambda shape: pl.BlockSpec((1,) + shape, lambda b, s, pt: (b, 0, 0))
    return pl.pallas_call(
        _attn_sample_kernel,
        grid_spec=pltpu.PrefetchScalarGridSpec(
            num_scalar_prefetch=1,
            grid=(Bs, n_pages // G),
            in_specs=[per_b((nq, D_ATTN)), per_b((D_ATTN, PAGE_SIZE)), per_b((D_ATTN, PAGE_SIZE)),
                      pl.BlockSpec((rows, 1), lambda b, s, pt: (0, 0))]
                     + [page_spec(j) for j in range(G)] * 2,
            out_specs=per_b((nq, D_ATTN)),
            scratch_shapes=[pltpu.VMEM((rows, D_ATTN), F32), pltpu.VMEM((rows, 1), F32)]),
        out_shape=jax.ShapeDtypeStruct((Bs, nq, D_ATTN), F32),
        compiler_params=_params("arbitrary", "arbitrary"),
        name="attn_sample",
    )(page_table, q, knew_t, vnew_t, bias_rows, *([cache_kt] * G), *([cache_vt] * G))


def _outproj_kernel(x_ref, a_ref, c_ref, ga_ref, sc_ref, sh_ref, g_ref, w_ref,
                    wr_hi_ref, wr_lo_ref, br_ref, x1_ref, h2_ref, lg_ref):
    mix = _dot(a_ref[0], w_ref[0:D_ATTN, :]) + _dot(c_ref[0], w_ref[D_ATTN:, :])
    x1 = x_ref[0] + ga_ref[0] * mix
    x1_ref[0] = x1
    ms = jnp.mean(x1 * x1, axis=-1, keepdims=True)
    h2 = x1 * lax.rsqrt(ms + EPS) * g_ref[...]
    h2 = h2 * (1.0 + sc_ref[0]) + sh_ref[0]
    _rows_to_tiles(h2_ref.at[0], h2)
    hi, lo = _split_bf16(h2)
    lg_ref[0] = (_dot(hi, wr_hi_ref[...]) + _dot(lo, wr_hi_ref[...])
                 + _dot(hi, wr_lo_ref[...]) + br_ref[...])


def _outproj(x, attn, conv, ga, sc, sh, g_norm, w_out_bf, wr_hi, wr_lo, br_pad):
    G, R, D = x.shape
    ts = min(ROW_TILE, R)
    tok = lambda n: pl.BlockSpec((1, ts, n), lambda g, t: (g, t, 0))
    const = lambda shape: pl.BlockSpec(shape, lambda g, t: (0, 0))
    return pl.pallas_call(
        _outproj_kernel,
        grid=(G, R // ts),
        in_specs=[tok(D), tok(D_ATTN), tok(CONV_CH), _mod_spec(ga, ts), _mod_spec(sc, ts),
                  _mod_spec(sh, ts), const((1, D)), const((D, D)),
                  const((D, LANES)), const((D, LANES)), const((1, LANES))],
        out_specs=[tok(D), pl.BlockSpec((1, ts * SUBLANES, LANES), lambda g, t: (g, t, 0)), tok(LANES)],
        out_shape=[jax.ShapeDtypeStruct((G, R, D), F32),
                   jax.ShapeDtypeStruct((G, R * SUBLANES, LANES), F32),
                   jax.ShapeDtypeStruct((G, R, LANES), F32)],
        compiler_params=_params("arbitrary", "arbitrary"),
        name="outproj",
    )(x, attn, conv, ga, sc, sh, g_norm.reshape(1, D), w_out_bf, wr_hi, wr_lo, br_pad)


def _route_kernel(lg_ref, dest_ref, gate_ref, cnt_ref, counts, offs, run):
    ph = pl.program_id(0)
    t = pl.program_id(1)
    tr = lg_ref.shape[0]
    lane = lax.broadcasted_iota(jnp.int32, (tr, LANES), 1).astype(F32)
    l = lg_ref[...]
    onehots, vals = [], []
    for _ in range(TOP_K):
        mx = jnp.max(l, axis=-1, keepdims=True)
        idx = jnp.min(jnp.where(l == mx, lane, float(LANES)), axis=-1, keepdims=True)
        oh = lane == idx
        onehots.append(oh)
        vals.append(mx)
        l = jnp.where(oh, NEG_BIG, l)
    total = jnp.zeros((tr, LANES), F32)
    for oh in onehots:
        total = total + jnp.where(oh, 1.0, 0.0)
    colsum = jnp.sum(total, axis=0, keepdims=True)

    @pl.when(jnp.logical_and(ph == 0, t == 0))
    def _():
        counts[...] = jnp.zeros_like(counts)

    @pl.when(ph == 0)
    def _():
        counts[...] += colsum

    @pl.when(jnp.logical_and(ph == 1, t == 0))
    def _():
        c = jnp.broadcast_to(counts[...], (8, LANES))
        c1 = c.astype(BF16)
        r1 = c - c1.astype(F32)
        c2 = r1.astype(BF16)
        c3 = (r1 - c2.astype(F32)).astype(BF16)
        rr = lax.broadcasted_iota(jnp.int32, (LANES, LANES), 0)
        cc = lax.broadcasted_iota(jnp.int32, (LANES, LANES), 1)
        before = jnp.where(rr < cc, 1.0, 0.0).astype(BF16)
        o = _dot(c1, before) + _dot(c2, before) + _dot(c3, before)
        offs[...] = o[0:1, :]
        run[...] = jnp.zeros_like(run)
        cnt_ref[...] = counts[...]

    @pl.when(ph == 1)
    def _():
        rr = lax.broadcasted_iota(jnp.int32, (tr, tr), 0)
        cc = lax.broadcasted_iota(jnp.int32, (tr, tr), 1)
        earlier = jnp.where(cc < rr, 1.0, 0.0).astype(BF16)
        pos = _dot(earlier, total.astype(BF16)) + (offs[...] + run[...])
        kcol = lax.broadcasted_iota(jnp.int32, (tr, LANES), 1)
        dest = jnp.zeros((tr, LANES), F32)
        gate = jnp.zeros((tr, LANES), F32)
        es = [jnp.exp(v - vals[0]) for v in vals]
        den = es[0] + es[1] + es[2] + es[3]
        for k in range(TOP_K):
            dk = jnp.sum(jnp.where(onehots[k], pos, 0.0), axis=-1, keepdims=True)
            dest = jnp.where(kcol == k, dk, dest)
            gate = jnp.where(kcol == k, es[k] / den, gate)
        dest_ref[...] = dest[:, 0:TOP_K].astype(jnp.int32)
        gate_ref[...] = gate[:, 0:TOP_K]
        run[...] += colsum


def _route(logits):
    T = logits.shape[0]
    tr = min(ROUTE_TILE, T)
    return pl.pallas_call(
        _route_kernel,
        grid=(2, T // tr),
        in_specs=[pl.BlockSpec((tr, LANES), lambda ph, t: (t, 0))],
        out_specs=[pl.BlockSpec((tr, TOP_K), lambda ph, t: (ph * t, 0)),
                   pl.BlockSpec((tr, TOP_K), lambda ph, t: (ph * t, 0)),
                   pl.BlockSpec((1, LANES), lambda ph, t: (0, 0))],
        out_shape=[jax.ShapeDtypeStruct((T, TOP_K), jnp.int32),
                   jax.ShapeDtypeStruct((T, TOP_K), F32),
                   jax.ShapeDtypeStruct((1, LANES), F32)],
        scratch_shapes=[pltpu.VMEM((1, LANES), F32)] * 3,
        compiler_params=_params("arbitrary", "arbitrary"),
        name="route",
    )(logits)


def _row_copy(src, dst, sem):
    return pltpu.make_async_copy(src, dst, sem)


def _dispatch_kernel(dest_ref, h_ref, xs_ref, sem):
    td = h_ref.shape[0] // SUBLANES

    def issue(t, _):
        for k in range(TOP_K):
            d = dest_ref[0, 0, t * TOP_K + k]
            _row_copy(_tile_of(h_ref, t), _tile_of(xs_ref, d), sem).start(priority=k % 2)
        return 0

    lax.fori_loop(0, td, issue, 0)

    def drain(t, _):
        for k in range(TOP_K):
            _row_copy(_tile_of(h_ref, 0), _tile_of(xs_ref, 0), sem).wait()
        return 0

    lax.fori_loop(0, td, drain, 0)


def _dispatch(h2, dest):
    T = h2.shape[0] // SUBLANES
    td = min(ROUTE_TILE, T)
    dest3 = dest.reshape(T // td, 1, td * TOP_K)
    return pl.pallas_call(
        _dispatch_kernel,
        grid=(T // td,),
        in_specs=[pl.BlockSpec((1, 1, td * TOP_K), lambda i: (i, 0, 0), memory_space=pltpu.SMEM),
                  pl.BlockSpec((td * SUBLANES, LANES), lambda i: (i, 0))],
        out_specs=pl.BlockSpec(memory_space=pl.ANY),
        out_shape=jax.ShapeDtypeStruct((T * TOP_K * SUBLANES, LANES), h2.dtype),
        scratch_shapes=[pltpu.SemaphoreType.DMA(())],
        compiler_params=_params("arbitrary"),
        name="dispatch",
    )(dest3, h2)


MOE_FF_CHUNK = 256


def _moe_kernel(blk_ref, exp_ref, lo_ref, hi_ref, first_ref, n_ref,
                x_ref, wgu_ref, bgu_ref, wd_ref, bd_ref, o_ref):
    w = pl.program_id(0)
    bm = x_ref.shape[0] // SUBLANES

    @pl.when(w < n_ref[0])
    def _():
        x = _tiles_to_rows(x_ref).astype(BF16)
        res = jnp.zeros((bm, D_MODEL), F32) + bd_ref[0]
        for n in range(D_FF // MOE_FF_CHUNK):
            c0 = n * MOE_FF_CHUNK
            c1 = c0 + MOE_FF_CHUNK
            g = _dot(x, wgu_ref[0, :, c0:c1]) + bgu_ref[0, :, c0:c1]
            u = _dot(x, wgu_ref[0, :, D_FF + c0:D_FF + c1]) + bgu_ref[0, :, D_FF + c0:D_FF + c1]
            g = jnp.minimum(g, SWIGLU_LIMIT)
            u = jnp.clip(u, -SWIGLU_LIMIT, SWIGLU_LIMIT)
            act = (u + 1.0) * (g * _sigmoid(SWIGLU_ALPHA * g))
            res = res + _dot(act.astype(BF16), wd_ref[0, c0:c1, :])

        @pl.when(first_ref[w] == 1)
        def _():
            _rows_to_tiles(o_ref, res)

        @pl.when(first_ref[w] == 0)
        def _():
            row = blk_ref[w] * bm + lax.broadcasted_iota(jnp.int32, (bm, 1), 0)
            mine = jnp.logical_and(row >= lo_ref[w], row < hi_ref[w])
            _rows_to_tiles(o_ref, jnp.where(mine, res, _tiles_to_rows(o_ref)))


def _moe_schedule(counts, n_rows, bm):
    c = counts.astype(jnp.int32)
    end = jnp.cumsum(c)
    start = end - c
    first_blk = start // bm
    last_blk = jnp.where(c > 0, (end - 1) // bm, first_blk - 1)
    n_items_e = last_blk - first_blk + 1
    item_end = jnp.cumsum(n_items_e)
    item_start = item_end - n_items_e
    n_items = item_end[-1]
    w_max = n_rows // bm + N_EXPERTS - 1
    w = jnp.minimum(jnp.arange(w_max, dtype=jnp.int32), n_items - 1)
    e = jnp.sum((item_end[None, :] <= w[:, None]).astype(jnp.int32), axis=1)
    blk = first_blk[e] + (w - item_start[e])
    prev = jnp.concatenate([jnp.full((1,), -1, jnp.int32), blk[:-1]])
    first = (blk != prev).astype(jnp.int32)
    return blk, e, start[e], end[e], first, n_items.reshape(1)


def _moe(xs, counts, wgu_bf, bgu, wd_bf, bd):
    R = xs.shape[0] // SUBLANES
    D = D_MODEL
    bm = min(MOE_TILE, R)
    blk, e, lo, hi, first, n_items = _moe_schedule(counts, R, bm)
    w_max = blk.shape[0]
    return pl.pallas_call(
        _moe_kernel,
        grid_spec=pltpu.PrefetchScalarGridSpec(
            num_scalar_prefetch=6,
            grid=(w_max,),
            in_specs=[pl.BlockSpec((bm * SUBLANES, LANES), lambda w, blk, e, *_: (blk[w], 0)),
                      pl.BlockSpec((1, D, 2 * D_FF), lambda w, blk, e, *_: (e[w], 0, 0)),
                      pl.BlockSpec((1, 1, 2 * D_FF), lambda w, blk, e, *_: (e[w], 0, 0)),
                      pl.BlockSpec((1, D_FF, D), lambda w, blk, e, *_: (e[w], 0, 0)),
                      pl.BlockSpec((1, 1, D), lambda w, blk, e, *_: (e[w], 0, 0))],
            out_specs=pl.BlockSpec((bm * SUBLANES, LANES), lambda w, blk, e, *_: (blk[w], 0))),
        out_shape=jax.ShapeDtypeStruct((R * SUBLANES, LANES), F32),
        compiler_params=_params("arbitrary"),
        name="moe",
    )(blk, e, lo, hi, first, n_items, xs, wgu_bf,
      bgu.reshape(N_EXPERTS, 1, 2 * D_FF), wd_bf, bd.reshape(N_EXPERTS, 1, D))


def _combine_kernel(dest_ref, gate_ref, x1_ref, ga_ref, g_ref, ys_ref, o_ref, buf, sem):
    tc = x1_ref.shape[1]

    def issue(t, _):
        for k in range(TOP_K):
            d = dest_ref[0, 0, 0, t * TOP_K + k]
            _row_copy(_tile_of(ys_ref, d), _tile_of(buf.at[k], t), sem).start(priority=k % 2)
        return 0

    lax.fori_loop(0, tc, issue, 0)

    def drain(t, _):
        for k in range(TOP_K):
            _row_copy(_tile_of(ys_ref, 0), _tile_of(buf.at[0], 0), sem).wait()
        return 0

    lax.fori_loop(0, tc, drain, 0)
    gate = gate_ref[0]
    ff = jnp.zeros((tc, D_MODEL), F32)
    for k in range(TOP_K):
        ff = ff + _tiles_to_rows(buf.at[k]) * gate[:, k:k + 1]
    x2 = x1_ref[0] + ga_ref[0] * ff
    ms = jnp.mean(x2 * x2, axis=-1, keepdims=True)
    o_ref[0] = x2 * lax.rsqrt(ms + EPS) * g_ref[...]


def _combine(ys, dest, gates, x1, ga, g_final):
    G, R, D = x1.shape
    tc = min(COMBINE_TILE, R)
    nt = R // tc
    dest4 = dest.reshape(G, nt, 1, tc * TOP_K)
    gates3 = gates.reshape(G, R, TOP_K)
    return pl.pallas_call(
        _combine_kernel,
        grid=(G, nt),
        in_specs=[pl.BlockSpec((1, 1, 1, tc * TOP_K), lambda g, t: (g, t, 0, 0),
                               memory_space=pltpu.SMEM),
                  pl.BlockSpec((1, tc, TOP_K), lambda g, t: (g, t, 0)),
                  pl.BlockSpec((1, tc, D), lambda g, t: (g, t, 0)),
                  _mod_spec(ga, tc),
                  pl.BlockSpec((1, D), lambda g, t: (0, 0)),
                  pl.BlockSpec(memory_space=pl.ANY)],
        out_specs=pl.BlockSpec((1, tc, D), lambda g, t: (g, t, 0)),
        out_shape=jax.ShapeDtypeStruct((G, R, D), F32),
        scratch_shapes=[pltpu.VMEM((TOP_K, tc * SUBLANES, LANES), F32), pltpu.SemaphoreType.DMA(())],
        compiler_params=_params("arbitrary", "arbitrary"),
        name="combine",
    )(dest4, gates3, x1, ga, g_final.reshape(1, D), ys)


def _ffn(x, attn, conv, mods, wts, g_final):
    ga1, sc2, sh2, ga2 = mods
    G, R, D = x.shape
    x1, h2, logits = _outproj(x, attn, conv, ga1, sc2, sh2, wts["g_ffn"], wts["w_out"],
                              wts["wr_hi"], wts["wr_lo"], wts["br"])
    T = G * R
    dest, gates, counts = _route(logits.reshape(T, LANES))
    xs = _dispatch(h2.reshape(T * SUBLANES, LANES), dest)
    ys = _moe(xs, counts[0, :N_EXPERTS], wts["wgu"], wts["bgu"], wts["wd"], wts["bd"])
    return _combine(ys, dest, gates, x1, ga2, g_final)


def kernel(x_prompt, x_sample, c_prompt, c_sample, cache_k, cache_v, state_conv, page_table,
           g_attn_norm, g_ffn_norm, w_ada, b_ada, w_in, sb_bias, w_dw, b_dw, ln_conv_g, ln_conv_b,
           w_out, w_router, b_router, w_gate_up, b_gate_up, w_down, b_down, g_final):
    assert w_ada.shape[0] == 1, "one trunk layer"
    B, S, D = x_prompt.shape
    Bs, nq, _ = x_sample.shape
    n_pool = cache_k.shape[1]

    wr = jnp.pad(w_router[0], ((0, 0), (0, LANES - N_EXPERTS)))
    wr_hi = wr.astype(BF16)
    wts = dict(
        g_ffn=g_ffn_norm[0], w_out=w_out[0].astype(BF16),
        wr_hi=wr_hi, wr_lo=(wr - wr_hi.astype(F32)).astype(BF16),
        br=jnp.pad(b_router[0], (0, LANES - N_EXPERTS), constant_values=NEG_BIG).reshape(1, LANES),
        wgu=w_gate_up[0].astype(BF16), bgu=b_gate_up[0],
        wd=w_down[0].astype(BF16), bd=b_down[0])
    w_in_bf = w_in[0].astype(BF16)

    ada = _ada(jnp.concatenate([c_prompt, c_sample], axis=0), w_ada[0], b_ada[0])
    ada_p = ada[:B].reshape(B, 1, 6, D)
    ada_s = jnp.broadcast_to(ada[B:].reshape(Bs, 1, 6, D), (Bs, nq, 6, D)).reshape(1, Bs * nq, 6, D)
    mod_p = [ada_p[:, :, i, :] for i in range(6)]
    mod_s = [ada_s[:, :, i, :] for i in range(6)]

    q, k, v, kb, vb, glu = _inproj(x_prompt, mod_p[1], mod_p[0], g_attn_norm[0], w_in_bf)
    attn = _attn_prompt(q, kb, vb, sb_bias[0])
    conv = _conv_prompt(glu, w_dw[0], b_dw[0], ln_conv_g[0], ln_conv_b[0])
    y_prompt = _ffn(x_prompt, attn, conv, (mod_p[2], mod_p[4], mod_p[3], mod_p[5]), wts, g_final)
    k_prompt = k.reshape(1, B, S, N_HEADS, HEAD_DIM)
    v_prompt = v.reshape(1, B, S, N_HEADS, HEAD_DIM)
    conv_prompt = glu[:, S - CONV_HIST:, :][None]

    xs = x_sample.reshape(1, Bs * nq, D)
    q_s, k_s, v_s, kb_s, vb_s, glu_s = _inproj(xs, mod_s[1], mod_s[0], g_attn_norm[0], w_in_bf)

    def new_page(a):
        a = jnp.transpose(a.reshape(Bs, nq, D_ATTN), (0, 2, 1))
        return jnp.pad(a, ((0, 0), (0, 0), (0, PAGE_SIZE - nq)))

    to_pages = lambda c: jnp.transpose(c[0], (0, 2, 3, 1))
    bias_rows = jnp.tile(sb_bias[0], nq).reshape(nq * N_HEADS, 1)
    attn_s = _attn_sample(q_s.reshape(Bs, nq, D_ATTN).astype(F32), new_page(kb_s), new_page(vb_s), bias_rows,
                          to_pages(cache_k), to_pages(cache_v), page_table)
    state_t = jnp.transpose(state_conv[0], (1, 0, 2))
    glu_t = jnp.transpose(glu_s.reshape(Bs, nq, CONV_CH), (1, 0, 2))
    conv_t, nstate_t = _conv_sample(state_t, glu_t, w_dw[0], b_dw[0], ln_conv_g[0], ln_conv_b[0])
    conv_s = jnp.transpose(conv_t, (1, 0, 2)).reshape(1, Bs * nq, CONV_CH)
    y_s = _ffn(xs, attn_s.reshape(1, Bs * nq, D_ATTN).astype(BF16), conv_s,
               (mod_s[2], mod_s[4], mod_s[3], mod_s[5]), wts, g_final)
    y_sample = y_s.reshape(Bs, nq, D)
    k_sample = k_s.reshape(1, Bs, nq, N_HEADS, HEAD_DIM)
    v_sample = v_s.reshape(1, Bs, nq, N_HEADS, HEAD_DIM)
    conv_sample = jnp.transpose(nstate_t, (1, 0, 2))[None]

    return (y_prompt, y_sample, k_prompt, v_prompt, conv_prompt, k_sample, v_sample, conv_sample)
```
